```python
import math
import jax, jax.numpy as jnp
from jax import lax
import numpy as np


D_MODEL = 1024
BATCH = 8
SEQ = 4096
DEPTH = 2

MEM_LEN = 256
DA_HEADS = 4
DA_QK_DIM = 64
DA_V_DIM = 2 * DA_QK_DIM
DA_QK_WIDTH = DA_HEADS * 2 * DA_QK_DIM
DA_WIDTH = DA_HEADS * DA_V_DIM
Q_BLOCK = 128
HG_HEADS = 4
HG_K = 128
HG_V = 128
HG_WIDTH = HG_HEADS * HG_K
HG_CHUNK = 64
CA_HEADS = 4
CA_DIM = 128
CA_WIDTH = CA_HEADS * CA_DIM
N_BRANCH = 3
IN_WIDTHS = (DA_QK_WIDTH, DA_QK_WIDTH, DA_WIDTH,
             HG_WIDTH, HG_WIDTH, HG_HEADS * HG_V, HG_HEADS * HG_V,
             CA_WIDTH, N_BRANCH * D_MODEL)
IN_TOTAL = sum(IN_WIDTHS)
N_EXPERTS = 16
N_GROUPS = 4
EXPERTS_PER_GROUP = N_EXPERTS // N_GROUPS
TOP_K = 2
D_EXPERT = 512
EPS = 1e-6

kernel_name = "hybrid_diffattn_hgrn2_memxattn_groupmoe"


def rmsnorm(x, g):
    xf = x.astype(jnp.float32)
    y = xf * lax.rsqrt(jnp.mean(xf * xf, axis=-1, keepdims=True) + EPS)
    return (y * g.astype(jnp.float32)).astype(x.dtype)


def alibi_slopes(n):
    return jnp.asarray([2.0 ** (-8.0 * (i + 1) / n) for i in range(n)], dtype=jnp.float32)


def diff_attention(q, k, v, lam, lam_init, sub_g):
    B, S = q.shape[0], q.shape[1]
    nb = S // Q_BLOCK
    slopes = alibi_slopes(DA_HEADS)
    scale = DA_QK_DIM ** -0.5
    kpos = jnp.arange(S)
    qb = q.reshape(B, nb, Q_BLOCK, DA_HEADS, 2, DA_QK_DIM).transpose(1, 0, 2, 3, 4, 5)

    def block(args):
        qi, bi = args
        qpos = bi * Q_BLOCK + jnp.arange(Q_BLOCK)
        s = jnp.einsum('bqhmd,bkhmd->bhmqk', qi, k,
                       preferred_element_type=jnp.float32) * scale
        dist = (qpos[:, None] - kpos[None, :]).astype(jnp.float32)
        bias = -slopes[:, None, None, None] * dist[None, None]
        s = jnp.where(qpos[:, None] >= kpos[None, :], s + bias, -jnp.inf)
        p = jax.nn.softmax(s, axis=-1)
        a = p[:, :, 0] - lam * p[:, :, 1]
        return jnp.einsum('bhqk,bkhd->bqhd', a.astype(v.dtype), v)

    o = lax.map(block, (qb, jnp.arange(nb)))
    o = o.transpose(1, 0, 2, 3, 4).reshape(B, S, DA_HEADS, DA_V_DIM)
    o = rmsnorm(o, sub_g) * (1.0 - lam_init)
    return o.reshape(B, S, DA_WIDTH)


def hgrn2(q, f_logit, i, g, lb, out_g):
    B, S = q.shape[0], q.shape[1]
    dt = i.dtype
    f = lb + (1.0 - lb) * jax.nn.sigmoid(f_logit.astype(jnp.float32))
    log_f = jnp.log(f)
    kk = 1.0 - f
    qq = jax.nn.silu(q.astype(jnp.float32))
    vv = i.astype(jnp.float32)
    C = HG_CHUNK
    nc = S // C

    def to_chunks(t):
        return t.reshape(B, nc, C, HG_HEADS, t.shape[-1]).transpose(1, 0, 3, 2, 4)

    qc, kc, vc, lfc = to_chunks(qq), to_chunks(kk), to_chunks(vv), to_chunks(log_f)
    causal = jnp.tril(jnp.ones((C, C), dtype=bool))

    def step(state, xs):
        qt, kt, vt, lft = xs
        bcum = jnp.cumsum(lft, axis=2)
        diff = bcum[:, :, :, None, :] - bcum[:, :, None, :, :]
        decay = jnp.exp(jnp.where(causal[:, :, None], diff, -jnp.inf))
        attn = jnp.einsum('bhtk,bhsk,bhtsk->bhts', qt, kt, decay)
        o = jnp.einsum('bhts,bhsv->bhtv', attn, vt)
        o = o + jnp.einsum('bhtk,bhkv->bhtv', qt * jnp.exp(bcum), state)
        btot = bcum[:, :, -1:, :]
        new_state = (jnp.exp(btot)[:, :, 0, :, None] * state
                     + jnp.einsum('bhsk,bhsv->bhkv', kt * jnp.exp(btot - bcum), vt))
        return new_state, o

    state0 = jnp.zeros((B, HG_HEADS, HG_K, HG_V), jnp.float32)
    _, o = lax.scan(step, state0, (qc, kc, vc, lfc))
    o = o.transpose(1, 0, 3, 2, 4).reshape(B, S, HG_HEADS, HG_V).astype(dt)
    o = rmsnorm(o, out_g) * jax.nn.silu(g)
    return o.reshape(B, S, HG_HEADS * HG_V)


def cross_attention(q, mk, mv):
    B, S = q.shape[0], q.shape[1]
    s = jnp.einsum('bshd,bmhd->bhsm', q, mk,
                   preferred_element_type=jnp.float32) * CA_DIM ** -0.5
    p = jax.nn.softmax(s, axis=-1)
    o = jnp.einsum('bhsm,bmhd->bshd', p.astype(mv.dtype), mv)
    return o.reshape(B, S, CA_WIDTH)


def grouped_moe(h, w_router, router_bias, w_gate, w_up, w_down):
    B, S, D = h.shape
    t = h.reshape(B * S, D)
    scores = jax.nn.sigmoid(jnp.dot(t, w_router, preferred_element_type=jnp.float32))
    sel = scores + router_bias.astype(jnp.float32)
    grp = sel.reshape(-1, N_GROUPS, EXPERTS_PER_GROUP)
    grp_score = jnp.sum(lax.top_k(grp, TOP_K)[0], axis=-1)
    best = jnp.argmax(grp_score, axis=-1)
    in_group = jnp.repeat(jax.nn.one_hot(best, N_GROUPS, dtype=jnp.bool_), EXPERTS_PER_GROUP, axis=-1)
    _, idx = lax.top_k(jnp.where(in_group, sel, -jnp.inf), TOP_K)
    wts = jnp.take_along_axis(scores, idx, axis=-1)
    wts = wts / jnp.sum(wts, axis=-1, keepdims=True)
    combine = jnp.sum(jax.nn.one_hot(idx, N_EXPERTS, dtype=jnp.float32) * wts[..., None], axis=1)
    combine = combine.astype(t.dtype)
    y = jnp.zeros_like(t)
    for e in range(N_EXPERTS):
        a = jax.nn.silu(t @ w_gate[e]) * (t @ w_up[e])
        y = y + (a * combine[:, e:e + 1]) @ w_down[e]
    return y.reshape(B, S, D)


def setup_inputs(seed: int = 0) -> dict:
    key = jax.random.key(seed)
    ks = jax.random.split(key, 20)
    f32 = jnp.float32
    D = D_MODEL
    n = lambda k, shape, s: jax.random.normal(k, shape, f32) * s
    return {
        "x": n(ks[0], (BATCH, SEQ, D), 1.0),
        "mem": n(ks[1], (BATCH, MEM_LEN, D), 1.0),
        "g_mix": 1.0 + n(ks[2], (DEPTH, D), 0.02),
        "w_in": n(ks[3], (DEPTH, D, IN_TOTAL), D ** -0.5),
        "da_lambda": n(ks[4], (DEPTH, 4, DA_QK_DIM), 0.1),
        "da_sub_g": 1.0 + n(ks[5], (DEPTH, DA_V_DIM), 0.02),
        "hg_lower_bounds": n(ks[6], (DEPTH, HG_WIDTH), 0.1),
        "hg_out_g": 1.0 + n(ks[7], (DEPTH, HG_V), 0.02),
        "g_mem": 1.0 + n(ks[8], (DEPTH, D), 0.02),
        "w_mem_kv": n(ks[9], (DEPTH, D, 2 * CA_WIDTH), D ** -0.5),
        "w_branch": n(ks[10], (DEPTH, N_BRANCH, DA_WIDTH, D), DA_WIDTH ** -0.5),
        "w_out": n(ks[11], (DEPTH, D, D), 0.5 * D ** -0.5),
        "g_ffn": 1.0 + n(ks[12], (DEPTH, D), 0.02),
        "w_router": n(ks[13], (D, N_EXPERTS), D ** -0.5),
        "router_bias": n(ks[14], (N_EXPERTS,), 0.01),
        "w_exp_gate": n(ks[15], (DEPTH, N_EXPERTS, D, D_EXPERT), D ** -0.5),
        "w_exp_up": n(ks[16], (DEPTH, N_EXPERTS, D, D_EXPERT), D ** -0.5),
        "w_exp_down": n(ks[17], (DEPTH, N_EXPERTS, D_EXPERT, D), 0.5 * D_EXPERT ** -0.5),
        "g_final": 1.0 + n(ks[18], (D,), 0.02),
    }


def reference(x, mem, g_mix, w_in, da_lambda, da_sub_g, hg_lower_bounds, hg_out_g,
              g_mem, w_mem_kv, w_branch, w_out, g_ffn, w_router, router_bias,
              w_exp_gate, w_exp_up, w_exp_down, g_final):
    B, S, D = x.shape
    points = []
    acc = 0
    for w in IN_WIDTHS[:-1]:
        acc += w
        points.append(acc)
    lbs = jnp.cumsum(jax.nn.softmax(hg_lower_bounds.astype(jnp.float32), axis=0), axis=0)
    lbs = lbs - lbs[0]
    for l in range(DEPTH):
        h = rmsnorm(x, g_mix[l])
        proj = h @ w_in[l]
        da_q, da_k, da_v, hg_q, hg_f, hg_i, hg_g, ca_q, gates = jnp.split(proj, points, axis=-1)
        lam_init = 0.8 - 0.6 * math.exp(-0.3 * l)
        lp = da_lambda[l].astype(jnp.float32)
        lam = jnp.exp(jnp.sum(lp[0] * lp[1])) - jnp.exp(jnp.sum(lp[2] * lp[3])) + lam_init
        y_a = diff_attention(da_q.reshape(B, S, DA_HEADS, 2, DA_QK_DIM),
                             da_k.reshape(B, S, DA_HEADS, 2, DA_QK_DIM),
                             da_v.reshape(B, S, DA_HEADS, DA_V_DIM),
                             lam, lam_init, da_sub_g[l])
        y_b = hgrn2(hg_q.reshape(B, S, HG_HEADS, HG_K),
                    hg_f.reshape(B, S, HG_HEADS, HG_K),
                    hg_i.reshape(B, S, HG_HEADS, HG_V),
                    hg_g.reshape(B, S, HG_HEADS, HG_V),
                    lbs[l].reshape(HG_HEADS, HG_K), hg_out_g[l])
        mkv = rmsnorm(mem, g_mem[l]) @ w_mem_kv[l]
        mk, mv = jnp.split(mkv, 2, axis=-1)
        y_c = cross_attention(ca_q.reshape(B, S, CA_HEADS, CA_DIM),
                              mk.reshape(B, -1, CA_HEADS, CA_DIM),
                              mv.reshape(B, -1, CA_HEADS, CA_DIM))
        gt = jax.nn.sigmoid(gates.reshape(B, S, N_BRANCH, D))
        merged = (gt[:, :, 0] * (y_a @ w_branch[l, 0])
                  + gt[:, :, 1] * (y_b @ w_branch[l, 1])
                  + gt[:, :, 2] * (y_c @ w_branch[l, 2]))
        x = x + merged @ w_out[l]
        x = x + grouped_moe(rmsnorm(x, g_ffn[l]), w_router, router_bias,
                            w_exp_gate[l], w_exp_up[l], w_exp_down[l])
    return rmsnorm(x, g_final)
```

```python
import functools
import math

import numpy as np
import jax
import jax.numpy as jnp
from jax import lax
from jax.experimental import pallas as pl
from jax.experimental.pallas import tpu as pltpu

F32 = jnp.float32
BF16 = jnp.bfloat16

D_MODEL = 1024
DEPTH = 2
DA_HEADS = 4
DA_QK_DIM = 64
HEAD_W = 128
MIX_W = 512
HG_HEADS = 4
HG_CHUNK = 64
CA_HEADS = 4
N_BRANCH = 3
IN_TOTAL = 8 * MIX_W + N_BRANCH * D_MODEL
N_EXPERTS = 16
N_GROUPS = 4
EXPERTS_PER_GROUP = 4
D_EXPERT = 512
EPS = 1e-6

COL_DA_Q, COL_DA_K, COL_DA_V, COL_HG_Q, COL_HG_F, COL_HG_I, COL_HG_G, COL_CA_Q = range(8)
COL_GATES = 8 * MIX_W

VMEM_LIMIT = 56 * 1024 * 1024

NEG_INF = float("-inf")


def _dot(a, b):
    return jnp.dot(a, b, preferred_element_type=F32)


def _dot_nt(a, b):
    return lax.dot_general(a, b, (((1,), (1,)), ((), ())), preferred_element_type=F32)


def _dot_tn(a, b):
    return lax.dot_general(a, b, (((0,), (0,)), ((), ())), preferred_element_type=F32)


def _rms(x, g):
    return x * lax.rsqrt(jnp.mean(x * x, axis=-1, keepdims=True) + EPS) * g


def _split_bf16(x):
    hi = x.astype(BF16)
    lo = (x - hi.astype(F32)).astype(BF16)
    return hi, lo


def _norm_proj_body(x_ref, g_ref, w_ref, o_ref, *, col_chunk):
    h = _rms(x_ref[...], g_ref[...]).astype(BF16)
    for c in range(o_ref.shape[1] // col_chunk):
        cs = slice(c * col_chunk, (c + 1) * col_chunk)
        o_ref[:, cs] = _dot(h, w_ref[:, cs]).astype(BF16)


def _norm_proj(x2d, g, w_bf16, *, row_tile, name):
    n, d = x2d.shape
    width = w_bf16.shape[1]
    return pl.pallas_call(
        functools.partial(_norm_proj_body, col_chunk=512),
        grid=(n // row_tile,),
        in_specs=[
            pl.BlockSpec((row_tile, d), lambda i: (i, 0)),
            pl.BlockSpec((1, d), lambda i: (0, 0)),
            pl.BlockSpec((d, width), lambda i: (0, 0), pipeline_mode=pl.Buffered(1)),
        ],
        out_specs=pl.BlockSpec((row_tile, width), lambda i: (i, 0)),
        out_shape=jax.ShapeDtypeStruct((n, width), BF16),
        compiler_params=pltpu.CompilerParams(
            dimension_semantics=("parallel",), vmem_limit_bytes=VMEM_LIMIT),
        name=name,
    )(x2d, g.reshape(1, d), w_bf16)


DA_TQ = 512
DA_TK = 256
DA_SCALE = DA_QK_DIM ** -0.5
ALIBI_SLOPES = tuple(2.0 ** (-8.0 * (i + 1) / DA_HEADS) for i in range(DA_HEADS))
assert all(math.frexp(s)[0] == 0.5 for s in ALIBI_SLOPES) and DA_TK <= 256


def _diff_attn_body(q_ref, k_ref, v_ref, lam_ref, subg_ref, o_ref, k0_ref, k1_ref,
                    *, lam_init):
    h = pl.program_id(1)
    qi = pl.program_id(2)
    seq = k_ref.shape[0]
    slope = jnp.where(h == 0, ALIBI_SLOPES[0],
                      jnp.where(h == 1, ALIBI_SLOPES[1],
                                jnp.where(h == 2, ALIBI_SLOPES[2], ALIBI_SLOPES[3]))).astype(F32)

    @pl.when(qi == 0)
    def _build_augmented_keys():
        k = k_ref[...].astype(F32)
        lane = lax.broadcasted_iota(jnp.int32, (seq, HEAD_W), 1)
        row = lax.broadcasted_iota(jnp.int32, (seq, HEAD_W), 0)
        local_bias = (row % DA_TK).astype(F32) * slope
        k0_ref[...] = jnp.where(lane < DA_QK_DIM, k,
                                jnp.where(lane == DA_QK_DIM, local_bias, 0.0)).astype(BF16)
        k1_ref[...] = jnp.where(lane >= DA_QK_DIM, k,
                                jnp.where(lane == 0, local_bias, 0.0)).astype(BF16)

    q = q_ref[...].astype(F32) * DA_SCALE
    lane_q = lax.broadcasted_iota(jnp.int32, (DA_TQ, HEAD_W), 1)
    q0 = jnp.where(lane_q < DA_QK_DIM, q, jnp.where(lane_q == DA_QK_DIM, 1.0, 0.0)).astype(BF16)
    q1 = jnp.where(lane_q >= DA_QK_DIM, q, jnp.where(lane_q == 0, 1.0, 0.0)).astype(BF16)

    def softmax_step(qa, ka, v, tile_bias, mask, carry):
        m, l, acc = carry
        s = _dot_nt(qa, ka)
        if mask is not None:
            s = jnp.where(mask, s, NEG_INF)
        m_new = jnp.maximum(m, jnp.max(s, axis=1, keepdims=True) + tile_bias)
        alpha = jnp.exp(m - m_new)
        p = jnp.exp(s - (m_new - tile_bias))
        l = alpha * l + jnp.sum(p, axis=1, keepdims=True)
        acc = alpha * acc + _dot(p.astype(BF16), v)
        return m_new, l, acc

    def kv_step(j, carry, mask):
        ks = pl.ds(pl.multiple_of(j * DA_TK, DA_TK), DA_TK)
        v = v_ref[ks, :]
        tile_bias = slope * (j * DA_TK).astype(F32)
        c0 = softmax_step(q0, k0_ref[ks, :], v, tile_bias, mask, carry[0])
        c1 = softmax_step(q1, k1_ref[ks, :], v, tile_bias, mask, carry[1])
        return c0, c1

    init_one = (jnp.full((DA_TQ, 1), NEG_INF, F32), jnp.zeros((DA_TQ, 1), F32),
                jnp.zeros((DA_TQ, HEAD_W), F32))
    tiles_per_q = DA_TQ // DA_TK
    carry = lax.fori_loop(0, qi * tiles_per_q, lambda j, c: kv_step(j, c, None),
                          (init_one, init_one))
    rq = lax.broadcasted_iota(jnp.int32, (DA_TQ, DA_TK), 0)
    ck = lax.broadcasted_iota(jnp.int32, (DA_TQ, DA_TK), 1)
    for d in range(tiles_per_q):
        carry = kv_step(qi * tiles_per_q + d, carry, rq >= ck + d * DA_TK)

    lp = lam_ref[...]
    lam = (jnp.exp(jnp.sum(lp[0:1] * lp[1:2], axis=1, keepdims=True))
           - jnp.exp(jnp.sum(lp[2:3] * lp[3:4], axis=1, keepdims=True)) + lam_init)
    (_, l0, a0), (_, l1, a1) = carry
    o = a0 / l0 - lam * (a1 / l1)
    o_ref[...] = (_rms(o, subg_ref[...]) * (1.0 - lam_init)).astype(BF16)


def _diff_attention(proj3, lam_params, sub_g, lam_init):
    b, s, _ = proj3.shape
    return pl.pallas_call(
        functools.partial(_diff_attn_body, lam_init=lam_init),
        grid=(b, DA_HEADS, s // DA_TQ),
        in_specs=[
            pl.BlockSpec((None, DA_TQ, HEAD_W), lambda bi, h, i: (bi, i, COL_DA_Q * 4 + h)),
            pl.BlockSpec((None, s, HEAD_W), lambda bi, h, i: (bi, 0, COL_DA_K * 4 + h)),
            pl.BlockSpec((None, s, HEAD_W), lambda bi, h, i: (bi, 0, COL_DA_V * 4 + h)),
            pl.BlockSpec((4, DA_QK_DIM), lambda bi, h, i: (0, 0)),
            pl.BlockSpec((1, HEAD_W), lambda bi, h, i: (0, 0)),
        ],
        out_specs=pl.BlockSpec((None, DA_TQ, HEAD_W), lambda bi, h, i: (bi, i, h)),
        out_shape=jax.ShapeDtypeStruct((b, s, MIX_W), BF16),
        scratch_shapes=[pltpu.VMEM((s, HEAD_W), BF16), pltpu.VMEM((s, HEAD_W), BF16)],
        compiler_params=pltpu.CompilerParams(
            dimension_semantics=("parallel", "parallel", "arbitrary"),
            vmem_limit_bytes=VMEM_LIMIT),
        name="diff_attention",
    )(proj3, proj3, proj3, lam_params, sub_g.reshape(1, HEAD_W))


HG_T = 512
HG_LEVELS = int(math.log2(HG_CHUNK))
HG_SUM_BLOCKS = HG_LEVELS + 2


def _hgrn_constants():
    c = HG_CHUNK
    r = np.arange(c)
    sums = np.zeros((HG_SUM_BLOCKS, c, c), np.float32)
    masks = np.zeros((HG_LEVELS + 1, c, c), np.float32)
    sums[0] = (r[None, :] <= r[:, None])
    for lev in range(HG_LEVELS):
        half = c >> (lev + 1)
        mid = (r // (2 * half)) * (2 * half) + half
        second = (r % (2 * half)) >= half
        rp = r[None, :]
        t_rows = second[:, None] & (rp >= mid[:, None]) & (rp <= r[:, None])
        s_rows = (~second)[:, None] & (rp > r[:, None]) & (rp < mid[:, None])
        sums[1 + lev] = t_rows | s_rows
        same_block = (r[:, None] // (2 * half)) == (r[None, :] // (2 * half))
        masks[lev] = same_block & second[:, None] & (~second)[None, :]
    sums[HG_LEVELS + 1] = (r[None, :] > r[:, None])
    masks[HG_LEVELS] = np.eye(c)
    return sums.reshape(HG_SUM_BLOCKS * c, c), masks


def _hgrn_body(q_ref, f_ref, i_ref, g_ref, lb_ref, outg_ref, sums_ref, masks_ref, o_ref,
               state_ref):
    @pl.when(pl.program_id(1) == 0)
    def _reset_state():
        state_ref[...] = jnp.zeros_like(state_ref)

    c = HG_CHUNK
    lb = lb_ref[...]
    sums = sums_ref[...]

    def chunk(ci, _):
        rows = pl.ds(pl.multiple_of(ci * c, c), c)
        f = lb + (1.0 - lb) * jax.nn.sigmoid(f_ref[rows, :].astype(F32))
        log_f = jnp.log(f)
        kk = 1.0 - f
        qq = jax.nn.silu(q_ref[rows, :].astype(F32))
        lf_hi, lf_lo = _split_bf16(log_f)
        decay = jnp.exp(_dot(sums, lf_hi) + _dot(sums, lf_lo))
        gate = jax.nn.silu(g_ref[rows, :].astype(F32))
        for h in range(HG_HEADS):
            cs = slice(h * HEAD_W, (h + 1) * HEAD_W)
            q_h, k_h, v_h = qq[:, cs], kk[:, cs], i_ref[rows, cs]
            e_cum = decay[0:c, cs]
            e_tail = decay[(HG_LEVELS + 1) * c:(HG_LEVELS + 2) * c, cs]
            attn = masks_ref[HG_LEVELS] * _dot_nt(q_h.astype(BF16), k_h.astype(BF16))
            for lev in range(HG_LEVELS):
                e = decay[(1 + lev) * c:(2 + lev) * c, cs]
                attn = attn + masks_ref[lev] * _dot_nt((q_h * e).astype(BF16),
                                                       (k_h * e).astype(BF16))
            st = state_ref[h]
            o = _dot(attn.astype(BF16), v_h) + _dot_nt((q_h * e_cum).astype(BF16),
                                                       st.astype(BF16))
            state_ref[h] = st * e_cum[c - 1:c, :] + _dot_tn(v_h, (k_h * e_tail).astype(BF16))
            o_ref[rows, cs] = (_rms(o, outg_ref[...]) * gate[:, cs]).astype(BF16)
        return 0

    lax.fori_loop(0, HG_T // c, chunk, 0)


def _hgrn2(proj3, lb, out_g):
    b, s, _ = proj3.shape
    sums, masks = _hgrn_constants()

    def col(cblk):
        return pl.BlockSpec((None, HG_T, MIX_W), lambda bi, t: (bi, t, cblk))

    def whole(shape):
        return pl.BlockSpec(shape, lambda bi, t: (0,) * len(shape))

    return pl.pallas_call(
        _hgrn_body,
        grid=(b, s // HG_T),
        in_specs=[col(COL_HG_Q), col(COL_HG_F), col(COL_HG_I), col(COL_HG_G),
                  whole((1, MIX_W)), whole((1, HEAD_W)),
                  whole(sums.shape), whole(masks.shape)],
        out_specs=pl.BlockSpec((None, HG_T, MIX_W), lambda bi, t: (bi, t, 0)),
        out_shape=jax.ShapeDtypeStruct((b, s, MIX_W), BF16),
        scratch_shapes=[pltpu.VMEM((HG_HEADS, HEAD_W, HEAD_W), F32)],
        compiler_params=pltpu.CompilerParams(
            dimension_semantics=("parallel", "arbitrary"), vmem_limit_bytes=VMEM_LIMIT),
        name="hgrn2",
    )(proj3, proj3, proj3, proj3, lb.reshape(1, MIX_W), out_g.reshape(1, HEAD_W),
      jnp.asarray(sums, BF16), jnp.asarray(masks, F32))


CA_TQ = 512


def _cross_attn_body(q_ref, mk_ref, mv_ref, o_ref):
    for h in range(CA_HEADS):
        cs = slice(h * HEAD_W, (h + 1) * HEAD_W)
        s = _dot_nt(q_ref[:, cs], mk_ref[:, cs]) * (HEAD_W ** -0.5)
        p = jnp.exp(s - jnp.max(s, axis=1, keepdims=True))
        l = jnp.sum(p, axis=1, keepdims=True)
        o_ref[:, cs] = (_dot(p.astype(BF16), mv_ref[:, cs]) / l).astype(BF16)


def _cross_attention(proj3, mkv3):
    b, s, _ = proj3.shape
    m = mkv3.shape[1]
    return pl.pallas_call(
        _cross_attn_body,
        grid=(b, s // CA_TQ),
        in_specs=[
            pl.BlockSpec((None, CA_TQ, MIX_W), lambda bi, i: (bi, i, COL_CA_Q)),
            pl.BlockSpec((None, m, MIX_W), lambda bi, i: (bi, 0, 0)),
            pl.BlockSpec((None, m, MIX_W), lambda bi, i: (bi, 0, 1)),
        ],
        out_specs=pl.BlockSpec((None, CA_TQ, MIX_W), lambda bi, i: (bi, i, 0)),
        out_shape=jax.ShapeDtypeStruct((b, s, MIX_W), BF16),
        compiler_params=pltpu.CompilerParams(
            dimension_semantics=("parallel", "parallel"), vmem_limit_bytes=VMEM_LIMIT),
        name="cross_attention",
    )(proj3, mkv3, mkv3)


MERGE_TM = 512
ROUTE_W = 128


def _top2_of_4(r0, r1, r2, r3):
    hi1, lo1 = jnp.maximum(r0, r1), jnp.minimum(r0, r1)
    hi2, lo2 = jnp.maximum(r2, r3), jnp.minimum(r2, r3)
    return jnp.maximum(hi1, hi2), jnp.maximum(jnp.minimum(hi1, hi2), jnp.maximum(lo1, lo2))


def _argmax_first(vals):
    best_v, best_i = vals[0], jnp.zeros_like(vals[0])
    for i in range(1, len(vals)):
        upd = vals[i] > best_v
        best_i = jnp.where(upd, float(i), best_i)
        best_v = jnp.where(upd, vals[i], best_v)
    return best_i


def _pick(idx, vals):
    out = vals[0]
    for i in range(1, len(vals)):
        out = jnp.where(idx == float(i), vals[i], out)
    return out


def _route(logits_t, bias_col):
    scores = jax.nn.sigmoid(logits_t)
    sel = scores + bias_col
    sel_rows = [sel[e:e + 1, :] for e in range(N_EXPERTS)]
    score_rows = [scores[e:e + 1, :] for e in range(N_EXPERTS)]
    grp = []
    for g in range(N_GROUPS):
        m1, m2 = _top2_of_4(*sel_rows[4 * g:4 * g + 4])
        grp.append(m1 + m2)
    best = _argmax_first(grp)
    cand = [_pick(best, [sel_rows[4 * g + i] for g in range(N_GROUPS)])
            for i in range(EXPERTS_PER_GROUP)]
    cand_score = [_pick(best, [score_rows[4 * g + i] for g in range(N_GROUPS)])
                  for i in range(EXPERTS_PER_GROUP)]
    i1 = _argmax_first(cand)
    i2 = _argmax_first([jnp.where(i1 == float(i), NEG_INF, cand[i])
                        for i in range(EXPERTS_PER_GROUP)])
    w1, w2 = _pick(i1, cand_score), _pick(i2, cand_score)
    tot = w1 + w2
    return w1 / tot, w2 / tot, best * 4.0 + i1, best * 4.0 + i2


def _merge_body(ya_ref, yb_ref, yc_ref, ga_ref, gb_ref, gc_ref, x_ref, wb_ref, wo_ref,
                gffn_ref, wr_hi_ref, wr_lo_ref, rbias_ref, x1_ref, h2_ref, route_ref):
    merged = None
    for i, (y_ref, gate_ref) in enumerate(((ya_ref, ga_ref), (yb_ref, gb_ref),
                                           (yc_ref, gc_ref))):
        gate = jax.nn.sigmoid(gate_ref[...].astype(F32))
        term = gate * _dot(y_ref[...], wb_ref[i])
        merged = term if merged is None else merged + term
    x1 = x_ref[...] + _dot(merged.astype(BF16), wo_ref[...])
    x1_ref[...] = x1
    h2 = _rms(x1, gffn_ref[...])
    h2_ref[...] = h2.astype(BF16)
    h_hi, h_lo = _split_bf16(h2)
    logits = _dot(h_hi, wr_hi_ref[...]) + (_dot(h_hi, wr_lo_ref[...]) + _dot(h_lo, wr_hi_ref[...]))
    logits_t = logits.T[0:N_EXPERTS, :]
    w1, w2, e1, e2 = _route(logits_t, rbias_ref[...])
    row = lax.broadcasted_iota(jnp.int32, (ROUTE_W, MERGE_TM), 0)
    packed = jnp.where(row == 0, w1, jnp.where(row == 1, w2,
                       jnp.where(row == 2, e1, jnp.where(row == 3, e2, 0.0))))
    route_ref[...] = packed.T


def _merge(ya, yb, yc, proj, x2d, w_branch, w_out, g_ffn, wr_hi, wr_lo, rbias):
    n = x2d.shape[0]
    tm = MERGE_TM

    def rows(width, cblk=0):
        return pl.BlockSpec((tm, width), lambda i: (i, cblk))

    def whole(shape):
        return pl.BlockSpec(shape, lambda i: (0,) * len(shape), pipeline_mode=pl.Buffered(1))

    return pl.pallas_call(
        _merge_body,
        grid=(n // tm,),
        in_specs=[rows(MIX_W), rows(MIX_W), rows(MIX_W),
                  rows(D_MODEL, COL_GATES // D_MODEL), rows(D_MODEL, COL_GATES // D_MODEL + 1),
                  rows(D_MODEL, COL_GATES // D_MODEL + 2),
                  rows(D_MODEL),
                  whole(w_branch.shape), whole(w_out.shape), whole((1, D_MODEL)),
                  whole(wr_hi.shape), whole(wr_lo.shape), whole((N_EXPERTS, 1))],
        out_specs=[rows(D_MODEL), rows(D_MODEL), rows(ROUTE_W)],
        out_shape=[jax.ShapeDtypeStruct((n, D_MODEL), F32),
                   jax.ShapeDtypeStruct((n, D_MODEL), BF16),
                   jax.ShapeDtypeStruct((n, ROUTE_W), F32)],
        compiler_params=pltpu.CompilerParams(
            dimension_semantics=("parallel",), vmem_limit_bytes=VMEM_LIMIT),
        name="merge_route",
    )(ya, yb, yc, proj, proj, proj, x2d, w_branch, w_out, g_ffn.reshape(1, D_MODEL), wr_hi, wr_lo,
      rbias.reshape(N_EXPERTS, 1))


MOE_TM = 1024


def _moe_body(h_ref, route_ref, x1_ref, wg_ref, wu_ref, wd_ref, gfin_ref, o_ref, acc_ref,
              *, final_norm):
    e = pl.program_id(1)
    h = h_ref[...]
    a = jax.nn.silu(_dot(h, wg_ref[...])) * _dot(h, wu_ref[...])
    ef = e.astype(F32)
    route = route_ref[...]
    weight = (jnp.where(route[:, 2:3] == ef, route[:, 0:1], 0.0)
              + jnp.where(route[:, 3:4] == ef, route[:, 1:2], 0.0))
    y = weight * _dot(a.astype(BF16), wd_ref[...])

    @pl.when(e == 0)
    def _first():
        acc_ref[...] = x1_ref[...] + y

    @pl.when(e > 0)
    def _rest():
        acc_ref[...] += y

    @pl.when(e == N_EXPERTS - 1)
    def _emit():
        out = acc_ref[...]
        o_ref[...] = _rms(out, gfin_ref[...]) if final_norm else out


def _moe(h2, route, x1, wg, wu, wd, g_final, final_norm):
    n = h2.shape[0]
    tm = MOE_TM
    return pl.pallas_call(
        functools.partial(_moe_body, final_norm=final_norm),
        grid=(n // tm, N_EXPERTS),
        in_specs=[
            pl.BlockSpec((tm, D_MODEL), lambda i, e: (i, 0)),
            pl.BlockSpec((tm, ROUTE_W), lambda i, e: (i, 0)),
            pl.BlockSpec((tm, D_MODEL), lambda i, e: (i, 0)),
            pl.BlockSpec((None, D_MODEL, D_EXPERT), lambda i, e: (e, 0, 0)),
            pl.BlockSpec((None, D_MODEL, D_EXPERT), lambda i, e: (e, 0, 0)),
            pl.BlockSpec((None, D_EXPERT, D_MODEL), lambda i, e: (e, 0, 0)),
            pl.BlockSpec((1, D_MODEL), lambda i, e: (0, 0)),
        ],
        out_specs=pl.BlockSpec((tm, D_MODEL), lambda i, e: (i, 0)),
        out_shape=jax.ShapeDtypeStruct((n, D_MODEL), F32),
        scratch_shapes=[pltpu.VMEM((tm, D_MODEL), F32)],
        compiler_params=pltpu.CompilerParams(
            dimension_semantics=("parallel", "arbitrary"), vmem_limit_bytes=VMEM_LIMIT),
        name="moe_experts",
    )(h2, route, x1, wg, wu, wd, g_final.reshape(1, D_MODEL))


def kernel(x, mem, g_mix, w_in, da_lambda, da_sub_g, hg_lower_bounds, hg_out_g, g_mem, w_mem_kv, w_branch, w_out, g_ffn, w_router, router_bias, w_exp_gate, w_exp_up, w_exp_down, g_final):
    b, s, d = x.shape
    m = mem.shape[1]
    n = b * s
    lbs = jnp.cumsum(jax.nn.softmax(hg_lower_bounds.astype(F32), axis=0), axis=0)
    lbs = lbs - lbs[0]
    wr_pad = jnp.pad(w_router.astype(F32), ((0, 0), (0, ROUTE_W - N_EXPERTS)))
    wr_hi, wr_lo = _split_bf16(wr_pad)
    x2d = x.reshape(n, d)
    mem2d = mem.reshape(b * m, d)
    for l in range(DEPTH):
        lam_init = 0.8 - 0.6 * math.exp(-0.3 * l)
        proj = _norm_proj(x2d, g_mix[l], w_in[l].astype(BF16), row_tile=512, name="in_proj")
        proj3 = proj.reshape(b, s, IN_TOTAL)
        mkv = _norm_proj(mem2d, g_mem[l], w_mem_kv[l].astype(BF16), row_tile=512,
                         name="mem_kv_proj")
        y_a = _diff_attention(proj3, da_lambda[l].astype(F32), da_sub_g[l], lam_init)
        y_b = _hgrn2(proj3, lbs[l], hg_out_g[l])
        y_c = _cross_attention(proj3, mkv.reshape(b, m, 2 * MIX_W))
        x1, h2, route = _merge(y_a.reshape(n, MIX_W), y_b.reshape(n, MIX_W),
                               y_c.reshape(n, MIX_W), proj, x2d,
                               w_branch[l].astype(BF16), w_out[l].astype(BF16), g_ffn[l],
                               wr_hi, wr_lo, router_bias.astype(F32))
        x2d = _moe(h2, route, x1, w_exp_gate[l].astype(BF16), w_exp_up[l].astype(BF16),
                   w_exp_down[l].astype(BF16), g_final, final_norm=(l == DEPTH - 1))
    return x2d.reshape(b, s, d)
```

```python
import functools
import math

import numpy as np
import jax
import jax.numpy as jnp
from jax import lax
from jax.experimental import pallas as pl
from jax.experimental.pallas import tpu as pltpu

F32 = jnp.float32
BF16 = jnp.bfloat16

D_MODEL = 1024
DEPTH = 2
DA_HEADS = 4
DA_QK_DIM = 64
HEAD_W = 128
MIX_W = 512
HG_HEADS = 4
HG_CHUNK = 64
CA_HEADS = 4
N_BRANCH = 3
IN_TOTAL = 8 * MIX_W + N_BRANCH * D_MODEL
N_EXPERTS = 16
N_GROUPS = 4
EXPERTS_PER_GROUP = 4
D_EXPERT = 512
EPS = 1e-6

COL_DA_Q, COL_DA_K, COL_DA_V, COL_HG_Q, COL_HG_F, COL_HG_I, COL_HG_G, COL_CA_Q = range(8)
COL_GATES = 8 * MIX_W

VMEM_LIMIT = 56 * 1024 * 1024

NEG_INF = float("-inf")


def _dot(a, b):
    return jnp.dot(a, b, preferred_element_type=F32)


def _dot_nt(a, b):
    return lax.dot_general(a, b, (((1,), (1,)), ((), ())), preferred_element_type=F32)


def _dot_tn(a, b):
    return lax.dot_general(a, b, (((0,), (0,)), ((), ())), preferred_element_type=F32)


def _rms(x, g):
    return x * lax.rsqrt(jnp.mean(x * x, axis=-1, keepdims=True) + EPS) * g


def _split_bf16(x):
    hi = x.astype(BF16)
    lo = (x - hi.astype(F32)).astype(BF16)
    return hi, lo


def _norm_proj_body(x_ref, g_ref, w_ref, o_ref, *, col_chunk):
    h = _rms(x_ref[...], g_ref[...]).astype(BF16)
    for c in range(o_ref.shape[1] // col_chunk):
        cs = slice(c * col_chunk, (c + 1) * col_chunk)
        o_ref[:, cs] = _dot(h, w_ref[:, cs]).astype(BF16)


def _norm_proj(x2d, g, w_bf16, *, row_tile, name):
    n, d = x2d.shape
    width = w_bf16.shape[1]
    return pl.pallas_call(
        functools.partial(_norm_proj_body, col_chunk=512),
        grid=(n // row_tile,),
        in_specs=[
            pl.BlockSpec((row_tile, d), lambda i: (i, 0)),
            pl.BlockSpec((1, d), lambda i: (0, 0)),
            pl.BlockSpec((d, width), lambda i: (0, 0), pipeline_mode=pl.Buffered(1)),
        ],
        out_specs=pl.BlockSpec((row_tile, width), lambda i: (i, 0)),
        out_shape=jax.ShapeDtypeStruct((n, width), BF16),
        compiler_params=pltpu.CompilerParams(
            dimension_semantics=("parallel",), vmem_limit_bytes=VMEM_LIMIT),
        name=name,
    )(x2d, g.reshape(1, d), w_bf16)


DA_TQ = 512
DA_TK = 256
DA_VROWS = HEAD_W + 16
LOG2E = math.log2(math.e)
DA_QSCALE = DA_QK_DIM ** -0.5 * LOG2E
ALIBI_SLOPES = tuple(2.0 ** (-8.0 * (i + 1) / DA_HEADS) for i in range(DA_HEADS))
assert all(math.frexp(s)[0] == 0.5 for s in ALIBI_SLOPES) and DA_TK <= 256


def _bf16_terms(x, n):
    terms, rest = [], np.float32(x)
    for _ in range(n):
        t = rest.astype(jnp.bfloat16)
        terms.append(float(t))
        rest = np.float32(rest - np.float32(t))
    return tuple(terms)


LOG2E_TERMS = _bf16_terms(LOG2E, 3)


def _own_half(idx, m):
    return idx < DA_QK_DIM if m == 0 else idx >= DA_QK_DIM


def _bias_base(m):
    return DA_QK_DIM * (1 - m)


def _diff_attn_body(q_ref, k_ref, v_ref, lam_ref, subg_ref, o_ref, k0_ref, k1_ref, vt_ref,
                    s_ref, max_ref, acc_ref, *, lam_init):
    h = pl.program_id(1)
    qi = pl.program_id(2)
    seq = k_ref.shape[0]
    slope = jnp.where(h == 0, ALIBI_SLOPES[0],
                      jnp.where(h == 1, ALIBI_SLOPES[1],
                                jnp.where(h == 2, ALIBI_SLOPES[2], ALIBI_SLOPES[3]))).astype(F32)

    @pl.when(qi == 0)
    def _prepare_keys_and_values():
        k = k_ref[...].astype(F32)
        lane = lax.broadcasted_iota(jnp.int32, (seq, HEAD_W), 1)
        row = lax.broadcasted_iota(jnp.int32, (seq, HEAD_W), 0)
        local_bias = (row & (DA_TK - 1)).astype(F32) * slope
        for m, ka_ref in ((0, k0_ref), (1, k1_ref)):
            base = _bias_base(m)
            slot = (lane >= base) & (lane < base + len(LOG2E_TERMS))
            ka_ref[...] = jnp.where(_own_half(lane, m), k,
                                    jnp.where(slot, local_bias, 0.0)).astype(BF16)
        for j in range(seq // DA_TK):
            vt_ref[j, 0:HEAD_W, :] = (
                v_ref[j * DA_TK:(j + 1) * DA_TK, :].astype(F32).T.astype(BF16))
            vt_ref[j, HEAD_W:DA_VROWS, :] = jnp.ones((DA_VROWS - HEAD_W, DA_TK), BF16)

    qt = (q_ref[...].astype(F32) * DA_QSCALE).T
    feat = lax.broadcasted_iota(jnp.int32, (HEAD_W, DA_TQ), 0)

    def q_aug(m):
        side = jnp.zeros((HEAD_W, DA_TQ), F32)
        for i, c in enumerate(LOG2E_TERMS):
            side = jnp.where(feat == _bias_base(m) + i, c, side)
        return jnp.where(_own_half(feat, m), qt, side).astype(BF16)

    q0t, q1t = q_aug(0), q_aug(1)

    def start_scores(j, slot):
        ks = pl.ds(pl.multiple_of(j * DA_TK, DA_TK), DA_TK)
        s_ref[slot, 0] = _dot(k0_ref[ks, :], q0t)
        s_ref[slot, 1] = _dot(k1_ref[ks, :], q1t)

    def consume(j, slot, mask):
        vt = vt_ref[j]
        tile_bias = (slope * LOG2E) * (j * DA_TK).astype(F32)
        for m in range(2):
            s = s_ref[slot, m]
            if mask is not None:
                s = jnp.where(mask, s, NEG_INF)
            m_old = max_ref[m]
            m_new = jnp.maximum(m_old, jnp.max(s, axis=0, keepdims=True) + tile_bias)
            alpha = jnp.exp2(m_old - m_new)
            p = jnp.exp2(s - (m_new - tile_bias))
            max_ref[m] = m_new
            acc_ref[m] = alpha * acc_ref[m] + _dot(vt, p.astype(BF16))

    max_ref[...] = jnp.full(max_ref.shape, NEG_INF, F32)
    acc_ref[...] = jnp.zeros_like(acc_ref)

    tiles_per_q = DA_TQ // DA_TK
    assert tiles_per_q == 2
    n_full = qi * tiles_per_q
    start_scores(0, 0)

    def full_tile_pair(jj, _):
        j = jj * 2
        start_scores(j + 1, 1)
        consume(j, 0, None)
        start_scores(j + 2, 0)
        consume(j + 1, 1, None)
        return 0

    lax.fori_loop(0, qi, full_tile_pair, 0)
    rk = lax.broadcasted_iota(jnp.int32, (DA_TK, DA_TQ), 0)
    cq = lax.broadcasted_iota(jnp.int32, (DA_TK, DA_TQ), 1)
    start_scores(n_full + 1, 1)
    consume(n_full, 0, cq >= rk)
    consume(n_full + 1, 1, cq >= rk + DA_TK)

    lp = lam_ref[...]
    lam = (jnp.exp(jnp.sum(lp[0:1] * lp[1:2], axis=1, keepdims=True))
           - jnp.exp(jnp.sum(lp[2:3] * lp[3:4], axis=1, keepdims=True)) + lam_init)
    a0, l0 = acc_ref[0, 0:HEAD_W, :], acc_ref[0, HEAD_W:HEAD_W + 1, :]
    a1, l1 = acc_ref[1, 0:HEAD_W, :], acc_ref[1, HEAD_W:HEAD_W + 1, :]
    o_t = a0 / l0 - lam * (a1 / l1)
    y_t = o_t * lax.rsqrt(jnp.mean(o_t * o_t, axis=0, keepdims=True) + EPS) * subg_ref[...]
    o_ref[...] = (y_t * (1.0 - lam_init)).T.astype(BF16)


def _diff_attention(proj3, lam_params, sub_g, lam_init):
    b, s, _ = proj3.shape
    return pl.pallas_call(
        functools.partial(_diff_attn_body, lam_init=lam_init),
        grid=(b, DA_HEADS, s // DA_TQ),
        in_specs=[
            pl.BlockSpec((None, DA_TQ, HEAD_W), lambda bi, h, i: (bi, i, COL_DA_Q * 4 + h)),
            pl.BlockSpec((None, s, HEAD_W), lambda bi, h, i: (bi, 0, COL_DA_K * 4 + h)),
            pl.BlockSpec((None, s, HEAD_W), lambda bi, h, i: (bi, 0, COL_DA_V * 4 + h)),
            pl.BlockSpec((4, DA_QK_DIM), lambda bi, h, i: (0, 0)),
            pl.BlockSpec((HEAD_W, 1), lambda bi, h, i: (0, 0)),
        ],
        out_specs=pl.BlockSpec((None, DA_TQ, HEAD_W), lambda bi, h, i: (bi, i, h)),
        out_shape=jax.ShapeDtypeStruct((b, s, MIX_W), BF16),
        scratch_shapes=[pltpu.VMEM((s, HEAD_W), BF16), pltpu.VMEM((s, HEAD_W), BF16),
                        pltpu.VMEM((s // DA_TK, DA_VROWS, DA_TK), BF16),
                        pltpu.VMEM((2, 2, DA_TK, DA_TQ), F32),
                        pltpu.VMEM((2, 1, DA_TQ), F32),
                        pltpu.VMEM((2, DA_VROWS, DA_TQ), F32)],
        compiler_params=pltpu.CompilerParams(
            dimension_semantics=("parallel", "parallel", "arbitrary"),
            vmem_limit_bytes=VMEM_LIMIT),
        name="diff_attention",
    )(proj3, proj3, proj3, lam_params, sub_g.reshape(HEAD_W, 1))


HG_T = 512
HG_LEVELS = int(math.log2(HG_CHUNK))
HG_SUM_BLOCKS = HG_LEVELS + 2


def _hgrn_constants():
    c = HG_CHUNK
    r = np.arange(c)
    sums = np.zeros((HG_SUM_BLOCKS, c, c), np.float32)
    masks = np.zeros((HG_LEVELS + 1, c, c), np.float32)
    sums[0] = (r[None, :] <= r[:, None])
    for lev in range(HG_LEVELS):
        half = c >> (lev + 1)
        mid = (r // (2 * half)) * (2 * half) + half
        second = (r % (2 * half)) >= half
        rp = r[None, :]
        t_rows = second[:, None] & (rp >= mid[:, None]) & (rp <= r[:, None])
        s_rows = (~second)[:, None] & (rp > r[:, None]) & (rp < mid[:, None])
        sums[1 + lev] = t_rows | s_rows
        same_block = (r[:, None] // (2 * half)) == (r[None, :] // (2 * half))
        masks[lev] = same_block & second[:, None] & (~second)[None, :]
    sums[HG_LEVELS + 1] = (r[None, :] > r[:, None])
    masks[HG_LEVELS] = np.eye(c)
    return sums.reshape(HG_SUM_BLOCKS * c, c), masks


def _layer_lower_bound(raw, layer):
    rows = [raw[i:i + 1] for i in range(raw.shape[0])]
    top = functools.reduce(jnp.maximum, rows)
    ex = [jnp.exp(r - top) for r in rows]
    tot = functools.reduce(jnp.add, ex)
    sm = [e / tot for e in ex]
    return functools.reduce(jnp.add, sm[:layer + 1]) - sm[0]


def _hgrn_body(q_ref, f_ref, i_ref, g_ref, lb_ref, outg_ref, sums_ref, masks_ref, o_ref,
               state_ref, *, layer):
    @pl.when(pl.program_id(1) == 0)
    def _reset_state():
        state_ref[...] = jnp.zeros_like(state_ref)

    c = HG_CHUNK
    lb = _layer_lower_bound(lb_ref[...], layer)
    sums = sums_ref[...]

    def chunk(ci, _):
        rows = pl.ds(pl.multiple_of(ci * c, c), c)
        f = lb + (1.0 - lb) * jax.nn.sigmoid(f_ref[rows, :].astype(F32))
        log_f = jnp.log(f)
        kk = 1.0 - f
        qq = jax.nn.silu(q_ref[rows, :].astype(F32))
        lf_hi, lf_lo = _split_bf16(log_f)
        decay = jnp.exp(_dot(sums, lf_hi) + _dot(sums, lf_lo))
        gate = jax.nn.silu(g_ref[rows, :].astype(F32))
        heads = [slice(h * HEAD_W, (h + 1) * HEAD_W) for h in range(HG_HEADS)]
        e_cum = decay[0:c, :]
        e_tail = decay[(HG_LEVELS + 1) * c:(HG_LEVELS + 2) * c, :]
        attn = []
        for cs in heads:
            a = masks_ref[HG_LEVELS] * _dot_nt(qq[:, cs].astype(BF16), kk[:, cs].astype(BF16))
            for lev in range(HG_LEVELS):
                e = decay[(1 + lev) * c:(2 + lev) * c, cs]
                a = a + masks_ref[lev] * _dot_nt((qq[:, cs] * e).astype(BF16),
                                                 (kk[:, cs] * e).astype(BF16))
            attn.append(a)
        carried = [_dot_nt((qq[:, cs] * e_cum[:, cs]).astype(BF16), state_ref[h].astype(BF16))
                   for h, cs in enumerate(heads)]
        update = [_dot_tn(i_ref[rows, cs], (kk[:, cs] * e_tail[:, cs]).astype(BF16))
                  for cs in heads]
        for h, cs in enumerate(heads):
            o = _dot(attn[h].astype(BF16), i_ref[rows, cs]) + carried[h]
            state_ref[h] = state_ref[h] * e_cum[c - 1:c, cs] + update[h]
            o_ref[rows, cs] = (_rms(o, outg_ref[...]) * gate[:, cs]).astype(BF16)
        return 0

    lax.fori_loop(0, HG_T // c, chunk, 0, unroll=2)


def _hgrn2(proj3, lower_bounds, out_g, layer):
    b, s, _ = proj3.shape
    sums, masks = _hgrn_constants()

    def col(cblk):
        return pl.BlockSpec((None, HG_T, MIX_W), lambda bi, t: (bi, t, cblk))

    def whole(shape):
        return pl.BlockSpec(shape, lambda bi, t: (0,) * len(shape))

    return pl.pallas_call(
        functools.partial(_hgrn_body, layer=layer),
        grid=(b, s // HG_T),
        in_specs=[col(COL_HG_Q), col(COL_HG_F), col(COL_HG_I), col(COL_HG_G),
                  whole(lower_bounds.shape), whole((1, HEAD_W)),
                  whole(sums.shape), whole(masks.shape)],
        out_specs=pl.BlockSpec((None, HG_T, MIX_W), lambda bi, t: (bi, t, 0)),
        out_shape=jax.ShapeDtypeStruct((b, s, MIX_W), BF16),
        scratch_shapes=[pltpu.VMEM((HG_HEADS, HEAD_W, HEAD_W), F32)],
        compiler_params=pltpu.CompilerParams(
            dimension_semantics=("parallel", "arbitrary"), vmem_limit_bytes=VMEM_LIMIT),
        name="hgrn2",
    )(proj3, proj3, proj3, proj3, lower_bounds, out_g.reshape(1, HEAD_W),
      jnp.asarray(sums, BF16), jnp.asarray(masks, F32))


CA_TQ = 512


def _cross_attn_body(q_ref, mk_ref, mv_ref, o_ref):
    for h in range(CA_HEADS):
        cs = slice(h * HEAD_W, (h + 1) * HEAD_W)
        s = _dot_nt(q_ref[:, cs], mk_ref[:, cs]) * (HEAD_W ** -0.5)
        p = jnp.exp(s - jnp.max(s, axis=1, keepdims=True))
        l = jnp.sum(p, axis=1, keepdims=True)
        o_ref[:, cs] = (_dot(p.astype(BF16), mv_ref[:, cs]) / l).astype(BF16)


def _cross_attention(proj3, mkv3):
    b, s, _ = proj3.shape
    m = mkv3.shape[1]
    return pl.pallas_call(
        _cross_attn_body,
        grid=(b, s // CA_TQ),
        in_specs=[
            pl.BlockSpec((None, CA_TQ, MIX_W), lambda bi, i: (bi, i, COL_CA_Q)),
            pl.BlockSpec((None, m, MIX_W), lambda bi, i: (bi, 0, 0)),
            pl.BlockSpec((None, m, MIX_W), lambda bi, i: (bi, 0, 1)),
        ],
        out_specs=pl.BlockSpec((None, CA_TQ, MIX_W), lambda bi, i: (bi, i, 0)),
        out_shape=jax.ShapeDtypeStruct((b, s, MIX_W), BF16),
        compiler_params=pltpu.CompilerParams(
            dimension_semantics=("parallel", "parallel"), vmem_limit_bytes=VMEM_LIMIT),
        name="cross_attention",
    )(proj3, mkv3, mkv3)


MERGE_TM = 512
ROUTE_W = 128


def _top2_of_4(r0, r1, r2, r3):
    hi1, lo1 = jnp.maximum(r0, r1), jnp.minimum(r0, r1)
    hi2, lo2 = jnp.maximum(r2, r3), jnp.minimum(r2, r3)
    return jnp.maximum(hi1, hi2), jnp.maximum(jnp.minimum(hi1, hi2), jnp.maximum(lo1, lo2))


def _argmax_first(vals):
    best_v, best_i = vals[0], jnp.zeros_like(vals[0])
    for i in range(1, len(vals)):
        upd = vals[i] > best_v
        best_i = jnp.where(upd, float(i), best_i)
        best_v = jnp.where(upd, vals[i], best_v)
    return best_i


def _pick(idx, vals):
    out = vals[0]
    for i in range(1, len(vals)):
        out = jnp.where(idx == float(i), vals[i], out)
    return out


def _route(logits_t, bias_col):
    scores = jax.nn.sigmoid(logits_t)
    sel = scores + bias_col
    sel_rows = [sel[e:e + 1, :] for e in range(N_EXPERTS)]
    score_rows = [scores[e:e + 1, :] for e in range(N_EXPERTS)]
    grp = []
    for g in range(N_GROUPS):
        m1, m2 = _top2_of_4(*sel_rows[4 * g:4 * g + 4])
        grp.append(m1 + m2)
    best = _argmax_first(grp)
    cand = [_pick(best, [sel_rows[4 * g + i] for g in range(N_GROUPS)])
            for i in range(EXPERTS_PER_GROUP)]
    cand_score = [_pick(best, [score_rows[4 * g + i] for g in range(N_GROUPS)])
                  for i in range(EXPERTS_PER_GROUP)]
    i1 = _argmax_first(cand)
    i2 = _argmax_first([jnp.where(i1 == float(i), NEG_INF, cand[i])
                        for i in range(EXPERTS_PER_GROUP)])
    w1, w2 = _pick(i1, cand_score), _pick(i2, cand_score)
    tot = w1 + w2
    return w1 / tot, w2 / tot, best * 4.0 + i1, best * 4.0 + i2


def _merge_body(ya_ref, yb_ref, yc_ref, ga_ref, gb_ref, gc_ref, x_ref, wb_ref, wo_ref,
                gffn_ref, wr_hi_ref, wr_lo_ref, rbias_ref, x1_ref, h2_ref, route_ref):
    merged = None
    for i, (y_ref, gate_ref) in enumerate(((ya_ref, ga_ref), (yb_ref, gb_ref),
                                           (yc_ref, gc_ref))):
        gate = jax.nn.sigmoid(gate_ref[...].astype(F32))
        term = gate * _dot(y_ref[...], wb_ref[i])
        merged = term if merged is None else merged + term
    x1 = x_ref[...] + _dot(merged.astype(BF16), wo_ref[...])
    x1_ref[...] = x1
    h2 = _rms(x1, gffn_ref[...])
    h2_ref[...] = h2.astype(BF16)
    h_hi, h_lo = _split_bf16(h2)
    logits = _dot(h_hi, wr_hi_ref[...]) + (_dot(h_hi, wr_lo_ref[...]) + _dot(h_lo, wr_hi_ref[...]))
    logits_t = logits.T[0:N_EXPERTS, :]
    w1, w2, e1, e2 = _route(logits_t, rbias_ref[...])
    row = lax.broadcasted_iota(jnp.int32, (ROUTE_W, MERGE_TM), 0)
    packed = jnp.where(row == 0, w1, jnp.where(row == 1, w2,
                       jnp.where(row == 2, e1, jnp.where(row == 3, e2, 0.0))))
    route_ref[...] = packed.T


def _merge(ya, yb, yc, proj, x2d, w_branch, w_out, g_ffn, wr_hi, wr_lo, rbias):
    n = x2d.shape[0]
    tm = MERGE_TM

    def rows(width, cblk=0):
        return pl.BlockSpec((tm, width), lambda i: (i, cblk))

    def whole(shape):
        return pl.BlockSpec(shape, lambda i: (0,) * len(shape), pipeline_mode=pl.Buffered(1))

    return pl.pallas_call(
        _merge_body,
        grid=(n // tm,),
        in_specs=[rows(MIX_W), rows(MIX_W), rows(MIX_W),
                  rows(D_MODEL, COL_GATES // D_MODEL), rows(D_MODEL, COL_GATES // D_MODEL + 1),
                  rows(D_MODEL, COL_GATES // D_MODEL + 2),
                  rows(D_MODEL),
                  whole(w_branch.shape), whole(w_out.shape), whole((1, D_MODEL)),
                  whole(wr_hi.shape), whole(wr_lo.shape), whole((N_EXPERTS, 1))],
        out_specs=[rows(D_MODEL), rows(D_MODEL), rows(ROUTE_W)],
        out_shape=[jax.ShapeDtypeStruct((n, D_MODEL), F32),
                   jax.ShapeDtypeStruct((n, D_MODEL), BF16),
                   jax.ShapeDtypeStruct((n, ROUTE_W), F32)],
        compiler_params=pltpu.CompilerParams(
            dimension_semantics=("parallel",), vmem_limit_bytes=VMEM_LIMIT),
        name="merge_route",
    )(ya, yb, yc, proj, proj, proj, x2d, w_branch, w_out, g_ffn.reshape(1, D_MODEL), wr_hi, wr_lo,
      rbias.reshape(N_EXPERTS, 1))


MOE_TM = 1024


def _moe_body(h_ref, route_ref, x1_ref, wg_ref, wu_ref, wd_ref, gfin_ref, o_ref, acc_ref,
              *, final_norm):
    e = pl.program_id(1)
    h = h_ref[...]
    a = jax.nn.silu(_dot(h, wg_ref[...])) * _dot(h, wu_ref[...])
    ef = e.astype(F32)
    route = route_ref[...]
    weight = (jnp.where(route[:, 2:3] == ef, route[:, 0:1], 0.0)
              + jnp.where(route[:, 3:4] == ef, route[:, 1:2], 0.0))
    y = weight * _dot(a.astype(BF16), wd_ref[...])

    @pl.when(e == 0)
    def _first():
        acc_ref[...] = x1_ref[...] + y

    @pl.when(e > 0)
    def _rest():
        acc_ref[...] += y

    @pl.when(e == N_EXPERTS - 1)
    def _emit():
        out = acc_ref[...]
        o_ref[...] = _rms(out, gfin_ref[...]) if final_norm else out


def _moe(h2, route, x1, wg, wu, wd, g_final, final_norm):
    n = h2.shape[0]
    tm = MOE_TM
    return pl.pallas_call(
        functools.partial(_moe_body, final_norm=final_norm),
        grid=(n // tm, N_EXPERTS),
        in_specs=[
            pl.BlockSpec((tm, D_MODEL), lambda i, e: (i, 0)),
            pl.BlockSpec((tm, ROUTE_W), lambda i, e: (i, 0)),
            pl.BlockSpec((tm, D_MODEL), lambda i, e: (i, 0)),
            pl.BlockSpec((None, D_MODEL, D_EXPERT), lambda i, e: (e, 0, 0)),
            pl.BlockSpec((None, D_MODEL, D_EXPERT), lambda i, e: (e, 0, 0)),
            pl.BlockSpec((None, D_EXPERT, D_MODEL), lambda i, e: (e, 0, 0)),
            pl.BlockSpec((1, D_MODEL), lambda i, e: (0, 0)),
        ],
        out_specs=pl.BlockSpec((tm, D_MODEL), lambda i, e: (i, 0)),
        out_shape=jax.ShapeDtypeStruct((n, D_MODEL), F32),
        scratch_shapes=[pltpu.VMEM((tm, D_MODEL), F32)],
        compiler_params=pltpu.CompilerParams(
            dimension_semantics=("parallel", "arbitrary"), vmem_limit_bytes=VMEM_LIMIT),
        name="moe_experts",
    )(h2, route, x1, wg, wu, wd, g_final.reshape(1, D_MODEL))


def kernel(x, mem, g_mix, w_in, da_lambda, da_sub_g, hg_lower_bounds, hg_out_g, g_mem, w_mem_kv, w_branch, w_out, g_ffn, w_router, router_bias, w_exp_gate, w_exp_up, w_exp_down, g_final):
    b, s, d = x.shape
    m = mem.shape[1]
    n = b * s
    wr_pad = jnp.pad(w_router.astype(F32), ((0, 0), (0, ROUTE_W - N_EXPERTS)))
    wr_hi, wr_lo = _split_bf16(wr_pad)
    x2d = x.reshape(n, d)
    mem2d = mem.reshape(b * m, d)
    for l in range(DEPTH):
        lam_init = 0.8 - 0.6 * math.exp(-0.3 * l)
        proj = _norm_proj(x2d, g_mix[l], w_in[l].astype(BF16), row_tile=512, name="in_proj")
        proj3 = proj.reshape(b, s, IN_TOTAL)
        mkv = _norm_proj(mem2d, g_mem[l], w_mem_kv[l].astype(BF16), row_tile=512,
                         name="mem_kv_proj")
        y_a = _diff_attention(proj3, da_lambda[l].astype(F32), da_sub_g[l], lam_init)
        y_b = _hgrn2(proj3, hg_lower_bounds.astype(F32), hg_out_g[l], l)
        y_c = _cross_attention(proj3, mkv.reshape(b, m, 2 * MIX_W))
        x1, h2, route = _merge(y_a.reshape(n, MIX_W), y_b.reshape(n, MIX_W),
                               y_c.reshape(n, MIX_W), proj, x2d,
                               w_branch[l].astype(BF16), w_out[l].astype(BF16), g_ffn[l],
                               wr_hi, wr_lo, router_bias.astype(F32))
        x2d = _moe(h2, route, x1, w_exp_gate[l].astype(BF16), w_exp_up[l].astype(BF16),
                   w_exp_down[l].astype(BF16), g_final, final_norm=(l == DEPTH - 1))
    return x2d.reshape(b, s, d)
```

```python
import functools
import math

import numpy as np
import jax
import jax.numpy as jnp
from jax import lax
from jax.experimental import pallas as pl
from jax.experimental.pallas import tpu as pltpu

F32 = jnp.float32
BF16 = jnp.bfloat16

D_MODEL = 1024
DEPTH = 2
DA_HEADS = 4
DA_QK_DIM = 64
HEAD_W = 128
MIX_W = 512
HG_HEADS = 4
HG_CHUNK = 64
CA_HEADS = 4
N_BRANCH = 3
IN_TOTAL = 8 * MIX_W + N_BRANCH * D_MODEL
N_EXPERTS = 16
N_GROUPS = 4
EXPERTS_PER_GROUP = 4
D_EXPERT = 512
EPS = 1e-6

COL_DA_Q, COL_DA_K, COL_DA_V, COL_HG_Q, COL_HG_F, COL_HG_I, COL_HG_G, COL_CA_Q = range(8)
COL_GATES = 8 * MIX_W

VMEM_LIMIT = 56 * 1024 * 1024

NEG_INF = float("-inf")


def _dot(a, b):
    return jnp.dot(a, b, preferred_element_type=F32)


def _dot_nt(a, b):
    return lax.dot_general(a, b, (((1,), (1,)), ((), ())), preferred_element_type=F32)


def _dot_tn(a, b):
    return lax.dot_general(a, b, (((0,), (0,)), ((), ())), preferred_element_type=F32)


def _rms(x, g):
    return x * lax.rsqrt(jnp.mean(x * x, axis=-1, keepdims=True) + EPS) * g


def _split_bf16(x):
    hi = x.astype(BF16)
    lo = (x - hi.astype(F32)).astype(BF16)
    return hi, lo


def _norm_proj_body(x_ref, g_ref, w_ref, o_ref, *, col_chunk):
    h = _rms(x_ref[...], g_ref[...]).astype(BF16)
    for c in range(o_ref.shape[1] // col_chunk):
        cs = slice(c * col_chunk, (c + 1) * col_chunk)
        o_ref[:, cs] = _dot(h, w_ref[:, cs]).astype(BF16)


def _norm_proj(x2d, g, w_bf16, *, row_tile, name):
    n, d = x2d.shape
    width = w_bf16.shape[1]
    return pl.pallas_call(
        functools.partial(_norm_proj_body, col_chunk=512),
        grid=(n // row_tile,),
        in_specs=[
            pl.BlockSpec((row_tile, d), lambda i: (i, 0)),
            pl.BlockSpec((1, d), lambda i: (0, 0)),
            pl.BlockSpec((d, width), lambda i: (0, 0), pipeline_mode=pl.Buffered(1)),
        ],
        out_specs=pl.BlockSpec((row_tile, width), lambda i: (i, 0)),
        out_shape=jax.ShapeDtypeStruct((n, width), BF16),
        compiler_params=pltpu.CompilerParams(
            dimension_semantics=("parallel",), vmem_limit_bytes=VMEM_LIMIT),
        name=name,
    )(x2d, g.reshape(1, d), w_bf16)


DA_TQ = 512
DA_TK = 256
DA_VROWS = HEAD_W + 16
LOG2E = math.log2(math.e)
DA_QSCALE = DA_QK_DIM ** -0.5 * LOG2E
ALIBI_SLOPES = tuple(2.0 ** (-8.0 * (i + 1) / DA_HEADS) for i in range(DA_HEADS))
assert all(math.frexp(s)[0] == 0.5 for s in ALIBI_SLOPES) and DA_TK <= 256


def _bf16_terms(x, n):
    terms, rest = [], np.float32(x)
    for _ in range(n):
        t = rest.astype(jnp.bfloat16)
        terms.append(float(t))
        rest = np.float32(rest - np.float32(t))
    return tuple(terms)


LOG2E_TERMS = _bf16_terms(LOG2E, 3)


def _own_half(idx, m):
    return idx < DA_QK_DIM if m == 0 else idx >= DA_QK_DIM


def _bias_base(m):
    return DA_QK_DIM * (1 - m)


def _diff_attn_body(q_ref, k_ref, v_ref, lam_ref, subg_ref, o_ref, k0_ref, k1_ref, vt_ref,
                    s_ref, max_ref, acc_ref, *, lam_init):
    h = pl.program_id(1)
    qi = pl.program_id(2)
    seq = k_ref.shape[0]
    slope = jnp.where(h == 0, ALIBI_SLOPES[0],
                      jnp.where(h == 1, ALIBI_SLOPES[1],
                                jnp.where(h == 2, ALIBI_SLOPES[2], ALIBI_SLOPES[3]))).astype(F32)

    @pl.when(qi == 0)
    def _prepare_keys_and_values():
        k = k_ref[...].astype(F32)
        lane = lax.broadcasted_iota(jnp.int32, (seq, HEAD_W), 1)
        row = lax.broadcasted_iota(jnp.int32, (seq, HEAD_W), 0)
        local_bias = (row & (DA_TK - 1)).astype(F32) * slope
        for m, ka_ref in ((0, k0_ref), (1, k1_ref)):
            base = _bias_base(m)
            slot = (lane >= base) & (lane < base + len(LOG2E_TERMS))
            ka_ref[...] = jnp.where(_own_half(lane, m), k,
                                    jnp.where(slot, local_bias, 0.0)).astype(BF16)
        for j in range(seq // DA_TK):
            vt_ref[j, 0:HEAD_W, :] = (
                v_ref[j * DA_TK:(j + 1) * DA_TK, :].astype(F32).T.astype(BF16))
            vt_ref[j, HEAD_W:DA_VROWS, :] = jnp.ones((DA_VROWS - HEAD_W, DA_TK), BF16)

    qt = (q_ref[...].astype(F32) * DA_QSCALE).T
    feat = lax.broadcasted_iota(jnp.int32, (HEAD_W, DA_TQ), 0)

    def q_aug(m):
        side = jnp.zeros((HEAD_W, DA_TQ), F32)
        for i, c in enumerate(LOG2E_TERMS):
            side = jnp.where(feat == _bias_base(m) + i, c, side)
        return jnp.where(_own_half(feat, m), qt, side).astype(BF16)

    q0t, q1t = q_aug(0), q_aug(1)

    def start_scores(j, slot):
        ks = pl.ds(pl.multiple_of(j * DA_TK, DA_TK), DA_TK)
        s_ref[slot, 0] = _dot(k0_ref[ks, :], q0t)
        s_ref[slot, 1] = _dot(k1_ref[ks, :], q1t)

    def consume(j, slot, mask):
        vt = vt_ref[j]
        tile_bias = (slope * LOG2E) * (j * DA_TK).astype(F32)
        for m in range(2):
            s = s_ref[slot, m]
            if mask is not None:
                s = jnp.where(mask, s, NEG_INF)
            m_old = max_ref[m]
            m_new = jnp.maximum(m_old, jnp.max(s, axis=0, keepdims=True) + tile_bias)
            alpha = jnp.exp2(m_old - m_new)
            p = jnp.exp2(s - (m_new - tile_bias))
            max_ref[m] = m_new
            acc_ref[m] = alpha * acc_ref[m] + _dot(vt, p.astype(BF16))

    max_ref[...] = jnp.full(max_ref.shape, NEG_INF, F32)
    acc_ref[...] = jnp.zeros_like(acc_ref)

    tiles_per_q = DA_TQ // DA_TK
    assert tiles_per_q == 2
    n_full = qi * tiles_per_q
    start_scores(0, 0)

    def full_tile_pair(jj, _):
        j = jj * 2
        start_scores(j + 1, 1)
        consume(j, 0, None)
        start_scores(j + 2, 0)
        consume(j + 1, 1, None)
        return 0

    lax.fori_loop(0, qi, full_tile_pair, 0)
    rk = lax.broadcasted_iota(jnp.int32, (DA_TK, DA_TQ), 0)
    cq = lax.broadcasted_iota(jnp.int32, (DA_TK, DA_TQ), 1)
    start_scores(n_full + 1, 1)
    consume(n_full, 0, cq >= rk)
    consume(n_full + 1, 1, cq >= rk + DA_TK)

    lp = lam_ref[...]
    lam = (jnp.exp(jnp.sum(lp[0:1] * lp[1:2], axis=1, keepdims=True))
           - jnp.exp(jnp.sum(lp[2:3] * lp[3:4], axis=1, keepdims=True)) + lam_init)
    a0, l0 = acc_ref[0, 0:HEAD_W, :], acc_ref[0, HEAD_W:HEAD_W + 1, :]
    a1, l1 = acc_ref[1, 0:HEAD_W, :], acc_ref[1, HEAD_W:HEAD_W + 1, :]
    o_t = a0 / l0 - lam * (a1 / l1)
    y_t = o_t * lax.rsqrt(jnp.mean(o_t * o_t, axis=0, keepdims=True) + EPS) * subg_ref[...]
    o_ref[...] = (y_t * (1.0 - lam_init)).T.astype(BF16)


def _diff_attention(proj3, lam_params, sub_g, lam_init):
    b, s, _ = proj3.shape
    return pl.pallas_call(
        functools.partial(_diff_attn_body, lam_init=lam_init),
        grid=(b, DA_HEADS, s // DA_TQ),
        in_specs=[
            pl.BlockSpec((None, DA_TQ, HEAD_W), lambda bi, h, i: (bi, i, COL_DA_Q * 4 + h)),
            pl.BlockSpec((None, s, HEAD_W), lambda bi, h, i: (bi, 0, COL_DA_K * 4 + h)),
            pl.BlockSpec((None, s, HEAD_W), lambda bi, h, i: (bi, 0, COL_DA_V * 4 + h)),
            pl.BlockSpec((4, DA_QK_DIM), lambda bi, h, i: (0, 0)),
            pl.BlockSpec((HEAD_W, 1), lambda bi, h, i: (0, 0)),
        ],
        out_specs=pl.BlockSpec((None, DA_TQ, HEAD_W), lambda bi, h, i: (bi, i, h)),
        out_shape=jax.ShapeDtypeStruct((b, s, MIX_W), BF16),
        scratch_shapes=[pltpu.VMEM((s, HEAD_W), BF16), pltpu.VMEM((s, HEAD_W), BF16),
                        pltpu.VMEM((s // DA_TK, DA_VROWS, DA_TK), BF16),
                        pltpu.VMEM((2, 2, DA_TK, DA_TQ), F32),
                        pltpu.VMEM((2, 1, DA_TQ), F32),
                        pltpu.VMEM((2, DA_VROWS, DA_TQ), F32)],
        compiler_params=pltpu.CompilerParams(
            dimension_semantics=("parallel", "parallel", "arbitrary"),
            vmem_limit_bytes=VMEM_LIMIT),
        name="diff_attention",
    )(proj3, proj3, proj3, lam_params, sub_g.reshape(HEAD_W, 1))


HG_T = 512
HG_LEVELS = int(math.log2(HG_CHUNK))
HG_SUM_BLOCKS = HG_LEVELS + 2


def _hgrn_constants():
    c = HG_CHUNK
    r = np.arange(c)
    sums = np.zeros((HG_SUM_BLOCKS, c, c), np.float32)
    masks = np.zeros((HG_LEVELS + 1, c, c), np.float32)
    sums[0] = (r[None, :] <= r[:, None])
    for lev in range(HG_LEVELS):
        half = c >> (lev + 1)
        mid = (r // (2 * half)) * (2 * half) + half
        second = (r % (2 * half)) >= half
        rp = r[None, :]
        t_rows = second[:, None] & (rp >= mid[:, None]) & (rp <= r[:, None])
        s_rows = (~second)[:, None] & (rp > r[:, None]) & (rp < mid[:, None])
        sums[1 + lev] = t_rows | s_rows
        same_block = (r[:, None] // (2 * half)) == (r[None, :] // (2 * half))
        masks[lev] = same_block & second[:, None] & (~second)[None, :]
    sums[HG_LEVELS + 1] = (r[None, :] > r[:, None])
    masks[HG_LEVELS] = np.eye(c)
    return sums.reshape(HG_SUM_BLOCKS * c, c), masks


def _layer_lower_bound(raw, layer):
    rows = [raw[i:i + 1] for i in range(raw.shape[0])]
    top = functools.reduce(jnp.maximum, rows)
    ex = [jnp.exp(r - top) for r in rows]
    tot = functools.reduce(jnp.add, ex)
    sm = [e / tot for e in ex]
    return functools.reduce(jnp.add, sm[:layer + 1]) - sm[0]


def _hgrn_body(q_ref, f_ref, i_ref, g_ref, lb_ref, outg_ref, sums_ref, masks_ref, o_ref,
               state_ref, *, layer):
    @pl.when(pl.program_id(1) == 0)
    def _reset_state():
        state_ref[...] = jnp.zeros_like(state_ref)

    c = HG_CHUNK
    lb = _layer_lower_bound(lb_ref[...], layer)
    sums = sums_ref[...]

    def chunk(ci, _):
        rows = pl.ds(pl.multiple_of(ci * c, c), c)
        f = lb + (1.0 - lb) * jax.nn.sigmoid(f_ref[rows, :].astype(F32))
        log_f = jnp.log(f)
        kk = 1.0 - f
        qq = jax.nn.silu(q_ref[rows, :].astype(F32))
        lf_hi, lf_lo = _split_bf16(log_f)
        decay = jnp.exp(_dot(sums, lf_hi) + _dot(sums, lf_lo))
        gate = jax.nn.silu(g_ref[rows, :].astype(F32))
        heads = [slice(h * HEAD_W, (h + 1) * HEAD_W) for h in range(HG_HEADS)]
        e_cum = decay[0:c, :]
        e_tail = decay[(HG_LEVELS + 1) * c:(HG_LEVELS + 2) * c, :]
        attn = []
        for cs in heads:
            a = masks_ref[HG_LEVELS] * _dot_nt(qq[:, cs].astype(BF16), kk[:, cs].astype(BF16))
            for lev in range(HG_LEVELS):
                e = decay[(1 + lev) * c:(2 + lev) * c, cs]
                a = a + masks_ref[lev] * _dot_nt((qq[:, cs] * e).astype(BF16),
                                                 (kk[:, cs] * e).astype(BF16))
            attn.append(a)
        carried = [_dot_nt((qq[:, cs] * e_cum[:, cs]).astype(BF16), state_ref[h].astype(BF16))
                   for h, cs in enumerate(heads)]
        update = [_dot_tn(i_ref[rows, cs], (kk[:, cs] * e_tail[:, cs]).astype(BF16))
                  for cs in heads]
        for h, cs in enumerate(heads):
            o = _dot(attn[h].astype(BF16), i_ref[rows, cs]) + carried[h]
            state_ref[h] = state_ref[h] * e_cum[c - 1:c, cs] + update[h]
            o_ref[rows, cs] = (_rms(o, outg_ref[...]) * gate[:, cs]).astype(BF16)
        return 0

    lax.fori_loop(0, HG_T // c, chunk, 0, unroll=2)


def _hgrn2(proj3, lower_bounds, out_g, layer):
    b, s, _ = proj3.shape
    sums, masks = _hgrn_constants()

    def col(cblk):
        return pl.BlockSpec((None, HG_T, MIX_W), lambda bi, t: (bi, t, cblk))

    def whole(shape):
        return pl.BlockSpec(shape, lambda bi, t: (0,) * len(shape))

    return pl.pallas_call(
        functools.partial(_hgrn_body, layer=layer),
        grid=(b, s // HG_T),
        in_specs=[col(COL_HG_Q), col(COL_HG_F), col(COL_HG_I), col(COL_HG_G),
                  whole(lower_bounds.shape), whole((1, HEAD_W)),
                  whole(sums.shape), whole(masks.shape)],
        out_specs=pl.BlockSpec((None, HG_T, MIX_W), lambda bi, t: (bi, t, 0)),
        out_shape=jax.ShapeDtypeStruct((b, s, MIX_W), BF16),
        scratch_shapes=[pltpu.VMEM((HG_HEADS, HEAD_W, HEAD_W), F32)],
        compiler_params=pltpu.CompilerParams(
            dimension_semantics=("parallel", "arbitrary"), vmem_limit_bytes=VMEM_LIMIT),
        name="hgrn2",
    )(proj3, proj3, proj3, proj3, lower_bounds, out_g.reshape(1, HEAD_W),
      jnp.asarray(sums, BF16), jnp.asarray(masks, F32))


CA_TQ = 512


def _cross_attn_body(q_ref, mk_ref, mv_ref, o_ref):
    for h in range(CA_HEADS):
        cs = slice(h * HEAD_W, (h + 1) * HEAD_W)
        s = _dot_nt(q_ref[:, cs], mk_ref[:, cs]) * (HEAD_W ** -0.5)
        p = jnp.exp(s - jnp.max(s, axis=1, keepdims=True))
        l = jnp.sum(p, axis=1, keepdims=True)
        o_ref[:, cs] = (_dot(p.astype(BF16), mv_ref[:, cs]) / l).astype(BF16)


def _cross_attention(proj3, mkv3):
    b, s, _ = proj3.shape
    m = mkv3.shape[1]
    return pl.pallas_call(
        _cross_attn_body,
        grid=(b, s // CA_TQ),
        in_specs=[
            pl.BlockSpec((None, CA_TQ, MIX_W), lambda bi, i: (bi, i, COL_CA_Q)),
            pl.BlockSpec((None, m, MIX_W), lambda bi, i: (bi, 0, 0)),
            pl.BlockSpec((None, m, MIX_W), lambda bi, i: (bi, 0, 1)),
        ],
        out_specs=pl.BlockSpec((None, CA_TQ, MIX_W), lambda bi, i: (bi, i, 0)),
        out_shape=jax.ShapeDtypeStruct((b, s, MIX_W), BF16),
        compiler_params=pltpu.CompilerParams(
            dimension_semantics=("parallel", "parallel"), vmem_limit_bytes=VMEM_LIMIT),
        name="cross_attention",
    )(proj3, mkv3, mkv3)


MERGE_TM = 512
ROUTE_W = 128


def _top2_of_4(r0, r1, r2, r3):
    hi1, lo1 = jnp.maximum(r0, r1), jnp.minimum(r0, r1)
    hi2, lo2 = jnp.maximum(r2, r3), jnp.minimum(r2, r3)
    return jnp.maximum(hi1, hi2), jnp.maximum(jnp.minimum(hi1, hi2), jnp.maximum(lo1, lo2))


def _argmax_first(vals):
    best_v, best_i = vals[0], jnp.zeros_like(vals[0])
    for i in range(1, len(vals)):
        upd = vals[i] > best_v
        best_i = jnp.where(upd, float(i), best_i)
        best_v = jnp.where(upd, vals[i], best_v)
    return best_i


def _pick(idx, vals):
    out = vals[0]
    for i in range(1, len(vals)):
        out = jnp.where(idx == float(i), vals[i], out)
    return out


def _route(logits_t, bias_col):
    scores = jax.nn.sigmoid(logits_t)
    sel = scores + bias_col
    sel_rows = [sel[e:e + 1, :] for e in range(N_EXPERTS)]
    score_rows = [scores[e:e + 1, :] for e in range(N_EXPERTS)]
    grp = []
    for g in range(N_GROUPS):
        m1, m2 = _top2_of_4(*sel_rows[4 * g:4 * g + 4])
        grp.append(m1 + m2)
    best = _argmax_first(grp)
    cand = [_pick(best, [sel_rows[4 * g + i] for g in range(N_GROUPS)])
            for i in range(EXPERTS_PER_GROUP)]
    cand_score = [_pick(best, [score_rows[4 * g + i] for g in range(N_GROUPS)])
                  for i in range(EXPERTS_PER_GROUP)]
    i1 = _argmax_first(cand)
    i2 = _argmax_first([jnp.where(i1 == float(i), NEG_INF, cand[i])
                        for i in range(EXPERTS_PER_GROUP)])
    w1, w2 = _pick(i1, cand_score), _pick(i2, cand_score)
    tot = w1 + w2
    return w1 / tot, w2 / tot, best * 4.0 + i1, best * 4.0 + i2


def _merge_body(ya_ref, yb_ref, yc_ref, ga_ref, gb_ref, gc_ref, x_ref, wb_ref, wo_ref,
                gffn_ref, wr_hi_ref, wr_lo_ref, rbias_ref, tri_ref,
                x1_ref, h2_ref, route_ref, route_rows_ref, counts_ref):
    merged = None
    for i, (y_ref, gate_ref) in enumerate(((ya_ref, ga_ref), (yb_ref, gb_ref),
                                           (yc_ref, gc_ref))):
        gate = jax.nn.sigmoid(gate_ref[...].astype(F32))
        term = gate * _dot(y_ref[...], wb_ref[i])
        merged = term if merged is None else merged + term
    x1 = x_ref[...] + _dot(merged.astype(BF16), wo_ref[...])
    x1_ref[...] = x1
    h2 = _rms(x1, gffn_ref[...])
    h2_ref[...] = h2.astype(BF16)
    h_hi, h_lo = _split_bf16(h2)
    logits = _dot(h_hi, wr_hi_ref[...]) + (_dot(h_hi, wr_lo_ref[...]) + _dot(h_lo, wr_hi_ref[...]))
    logits_t = logits.T[0:N_EXPERTS, :]
    w1, w2, e1, e2 = _route(logits_t, rbias_ref[...])
    expert = lax.broadcasted_iota(jnp.int32, (N_EXPERTS, MERGE_TM), 0).astype(F32)
    pick1 = jnp.where(expert == e1, 1.0, 0.0)
    pick2 = jnp.where(expert == e2, 1.0, 0.0)
    picked = pick1 + pick2
    before = _dot(picked.astype(BF16), tri_ref[...])
    r1 = jnp.sum(pick1 * before, axis=0, keepdims=True)
    r2 = jnp.sum(pick2 * before, axis=0, keepdims=True)
    counts_ref[...] = jnp.broadcast_to(jnp.sum(picked, axis=1, keepdims=True),
                                       counts_ref.shape)
    fields = (w1, w2, e1, e2, r1, r2)
    row = lax.broadcasted_iota(jnp.int32, (ROUTE_W, MERGE_TM), 0)
    packed = jnp.zeros((ROUTE_W, MERGE_TM), F32)
    for i, f in enumerate(fields):
        packed = jnp.where(row == i, f, packed)
    route_ref[...] = packed.T
    route_rows_ref[...] = packed[0:8, :]


def _merge(ya, yb, yc, proj, x2d, w_branch, w_out, g_ffn, wr_hi, wr_lo, rbias):
    n = x2d.shape[0]
    tm = MERGE_TM

    def rows(width, cblk=0):
        return pl.BlockSpec((tm, width), lambda i: (i, cblk))

    def whole(shape):
        return pl.BlockSpec(shape, lambda i: (0,) * len(shape), pipeline_mode=pl.Buffered(1))

    return pl.pallas_call(
        _merge_body,
        grid=(n // tm,),
        in_specs=[rows(MIX_W), rows(MIX_W), rows(MIX_W),
                  rows(D_MODEL, COL_GATES // D_MODEL), rows(D_MODEL, COL_GATES // D_MODEL + 1),
                  rows(D_MODEL, COL_GATES // D_MODEL + 2),
                  rows(D_MODEL),
                  whole(w_branch.shape), whole(w_out.shape), whole((1, D_MODEL)),
                  whole(wr_hi.shape), whole(wr_lo.shape), whole((N_EXPERTS, 1)),
                  whole((tm, tm))],
        out_specs=[rows(D_MODEL), rows(D_MODEL), rows(ROUTE_W),
                   pl.BlockSpec((None, 8, tm), lambda i: (i, 0, 0)),
                   pl.BlockSpec((None, N_EXPERTS, ROUTE_W), lambda i: (i, 0, 0))],
        out_shape=[jax.ShapeDtypeStruct((n, D_MODEL), F32),
                   jax.ShapeDtypeStruct((n, D_MODEL), BF16),
                   jax.ShapeDtypeStruct((n, ROUTE_W), F32),
                   jax.ShapeDtypeStruct((n // tm, 8, tm), F32),
                   jax.ShapeDtypeStruct((n // tm, N_EXPERTS, ROUTE_W), F32)],
        compiler_params=pltpu.CompilerParams(
            dimension_semantics=("parallel",), vmem_limit_bytes=VMEM_LIMIT),
        name="merge_route",
    )(ya, yb, yc, proj, proj, proj, x2d, w_branch, w_out, g_ffn.reshape(1, D_MODEL), wr_hi, wr_lo,
      rbias.reshape(N_EXPERTS, 1),
      jnp.asarray(np.triu(np.ones((tm, tm), np.float32), 1), BF16))


PIECE = 16
BLOCK_ROWS = -(-(2 * MERGE_TM + N_EXPERTS * (PIECE - 1)) // 256) * 256
BLOCK_PIECES = BLOCK_ROWS // PIECE
FFN_TM = 512
XS_W = D_MODEL + 128
TAIL_PIECES = FFN_TM // PIECE


def _max_ffn_tiles(n_blocks):
    rows = n_blocks * (2 * MERGE_TM + N_EXPERTS * (PIECE - 1)) + N_EXPERTS * (FFN_TM - PIECE)
    return -(-rows // FFN_TM)


def _dispatch_plan(counts_out):
    i32 = jnp.int32
    counts = counts_out[:, :, 0].astype(i32)
    n_blocks = counts.shape[0]
    padded = (counts + PIECE - 1) // PIECE * PIECE
    loc = jnp.cumsum(padded, axis=1) - padded
    tot = jnp.sum(padded, axis=0)
    region = (tot + FFN_TM - 1) // FFN_TM * FFN_TM
    off = jnp.cumsum(region) - region
    seg = off[None, :] + jnp.cumsum(padded, axis=0) - padded
    k_row = jnp.arange(BLOCK_PIECES, dtype=i32) * PIECE
    e_of = jnp.sum(k_row[None, :, None] >= (loc + padded)[:, None, :], axis=2, dtype=i32)
    e_of = jnp.minimum(e_of, N_EXPERTS - 1)
    dest = (jnp.take_along_axis(seg, e_of, axis=1) + k_row[None, :]
            - jnp.take_along_axis(loc, e_of, axis=1))
    tail_row = tot[:, None] + jnp.arange(TAIL_PIECES, dtype=i32)[None, :] * PIECE
    tail_valid = (tail_row < region[:, None]).reshape(-1)
    order = jnp.argsort(jnp.logical_not(tail_valid), stable=True)
    tile_row = jnp.arange(_max_ffn_tiles(n_blocks), dtype=i32) * FFN_TM
    tile_expert = jnp.sum(tile_row[:, None] >= (off + region)[None, :], axis=1, dtype=i32)
    return dict(
        loc=loc.reshape(-1), dest=dest.reshape(-1).astype(i32),
        n_pieces=(jnp.sum(padded, axis=1) // PIECE).astype(i32),
        zero_dest=(off[:, None] + tail_row).reshape(-1)[order].astype(i32),
        n_zero=jnp.sum(tail_valid, dtype=i32).reshape(1),
        tile_expert=jnp.minimum(tile_expert, N_EXPERTS - 1),
        n_tiles=(jnp.sum(region) // FFN_TM).astype(i32).reshape(1))


def _block_row(expert, rank, loc_ref, block):
    start = jnp.zeros_like(rank)
    for e in range(N_EXPERTS):
        start = jnp.where(expert == float(e), loc_ref[block * N_EXPERTS + e].astype(F32), start)
    return start + rank


def _weight_terms(w_col):
    hi = w_col.astype(BF16).astype(F32)
    mid = (w_col - hi).astype(BF16).astype(F32)
    lo = w_col - hi - mid
    lane = lax.broadcasted_iota(jnp.int32, (w_col.shape[0], 128), 1)
    return jnp.where(lane == 0, hi, jnp.where(lane == 1, mid,
                                              jnp.where(lane == 2, lo, 0.0))).astype(BF16)


def _for_each(n, fn):
    lax.fori_loop(0, n, lambda k, c: (fn(k), c)[1], 0)


def _dispatch_body(loc_ref, dest_ref, npieces_ref, zdest_ref, nzero_ref, ntiles_ref,
                   rows_ref, cols_ref, h_ref, xs_hbm, buf_ref, zero_ref, sem, zero_sem):
    b = pl.program_id(0)
    slot = b % 2

    def piece_copy(k, block, s):
        src = buf_ref.at[s, pl.ds(pl.multiple_of(k * PIECE, PIECE), PIECE), :]
        dst = xs_hbm.at[pl.ds(pl.multiple_of(dest_ref[block * BLOCK_PIECES + k], PIECE), PIECE), :]
        return pltpu.make_async_copy(src, dst, sem.at[s])

    def zero_copy(j):
        dst = xs_hbm.at[pl.ds(pl.multiple_of(zdest_ref[j], PIECE), PIECE), :]
        return pltpu.make_async_copy(zero_ref.at[pl.ds(0, PIECE), :], dst, zero_sem)

    def zero_tile_copy(t):
        dst = xs_hbm.at[pl.ds(pl.multiple_of(t * FFN_TM, FFN_TM), FFN_TM), :]
        return pltpu.make_async_copy(zero_ref, dst, zero_sem)

    idle_tiles = xs_hbm.shape[0] // FFN_TM - ntiles_ref[0]

    @pl.when(b == 0)
    def _start_zero_fill():
        zero_ref[...] = jnp.zeros_like(zero_ref)
        _for_each(nzero_ref[0], lambda j: zero_copy(j).start())
        _for_each(idle_tiles, lambda t: zero_tile_copy(ntiles_ref[0] + t).start())

    idx1 = _block_row(rows_ref[2:3, :], rows_ref[4:5, :], loc_ref, b)
    idx2 = _block_row(rows_ref[3:4, :], rows_ref[5:6, :], loc_ref, b)
    row = lax.broadcasted_iota(jnp.int32, (BLOCK_ROWS, MERGE_TM), 0).astype(F32)
    p1 = jnp.where(row == idx1, 1.0, 0.0).astype(BF16)
    p2 = jnp.where(row == idx2, 1.0, 0.0).astype(BF16)
    buf_ref[slot, :, 0:D_MODEL] = _dot(p1 + p2, h_ref[...]).astype(BF16)
    buf_ref[slot, :, D_MODEL:XS_W] = (_dot(p1, _weight_terms(cols_ref[:, 0:1]))
                                      + _dot(p2, _weight_terms(cols_ref[:, 1:2]))).astype(BF16)
    _for_each(npieces_ref[b], lambda k: piece_copy(k, b, slot).start())

    @pl.when(b > 0)
    def _wait_previous_block():
        _for_each(npieces_ref[b - 1], lambda k: piece_copy(k, b - 1, 1 - slot).wait())

    @pl.when(b == pl.num_programs(0) - 1)
    def _wait_last_block():
        _for_each(npieces_ref[b], lambda k: piece_copy(k, b, slot).wait())

    @pl.when(b == 0)
    def _wait_zero_fill():
        _for_each(nzero_ref[0], lambda j: zero_copy(j).wait())
        _for_each(idle_tiles, lambda t: zero_tile_copy(ntiles_ref[0] + t).wait())


def _dispatch(plan, route_rows, route, h2):
    n_blocks = route_rows.shape[0]
    rows_max = _max_ffn_tiles(n_blocks) * FFN_TM
    return pl.pallas_call(
        _dispatch_body,
        grid_spec=pltpu.PrefetchScalarGridSpec(
            num_scalar_prefetch=6,
            grid=(n_blocks,),
            in_specs=[pl.BlockSpec((None, 8, MERGE_TM), lambda i, *_: (i, 0, 0)),
                      pl.BlockSpec((MERGE_TM, ROUTE_W), lambda i, *_: (i, 0)),
                      pl.BlockSpec((MERGE_TM, D_MODEL), lambda i, *_: (i, 0))],
            out_specs=pl.BlockSpec(memory_space=pl.ANY),
            scratch_shapes=[pltpu.VMEM((2, BLOCK_ROWS, XS_W), BF16),
                            pltpu.VMEM((FFN_TM, XS_W), BF16),
                            pltpu.SemaphoreType.DMA((2,)),
                            pltpu.SemaphoreType.DMA(())]),
        out_shape=jax.ShapeDtypeStruct((rows_max, XS_W), BF16),
        compiler_params=pltpu.CompilerParams(
            dimension_semantics=("arbitrary",), vmem_limit_bytes=VMEM_LIMIT),
        name="moe_dispatch",
    )(plan["loc"], plan["dest"], plan["n_pieces"], plan["zero_dest"], plan["n_zero"],
      plan["n_tiles"], route_rows, route, h2)


def _ffn_body(tile_expert_ref, ntiles_ref, xs_ref, wg_ref, wu_ref, wd_ref, ys_ref):
    @pl.when(pl.program_id(0) >= ntiles_ref[0])
    def _idle_tile():
        ys_ref[...] = jnp.zeros_like(ys_ref)

    @pl.when(pl.program_id(0) < ntiles_ref[0])
    def _run_tile():
        x = xs_ref[:, 0:D_MODEL]
        terms = xs_ref[:, D_MODEL:XS_W].astype(F32)
        weight = terms[:, 0:1] + terms[:, 1:2] + terms[:, 2:3]
        a = jax.nn.silu(_dot(x, wg_ref[...])) * _dot(x, wu_ref[...])
        ys_ref[...] = _dot((a * weight).astype(BF16), wd_ref[...]).astype(BF16)


def _expert_ffn(plan, xs, wg, wu, wd):
    n_tiles_max = xs.shape[0] // FFN_TM

    def tile(i, tile_expert, n_tiles):
        return jnp.minimum(i, n_tiles[0] - 1)

    def expert(shape):
        return pl.BlockSpec((None,) + shape, lambda i, te, nt: (te[tile(i, te, nt)], 0, 0))

    return pl.pallas_call(
        _ffn_body,
        grid_spec=pltpu.PrefetchScalarGridSpec(
            num_scalar_prefetch=2,
            grid=(n_tiles_max,),
            in_specs=[pl.BlockSpec((FFN_TM, XS_W), lambda i, te, nt: (tile(i, te, nt), 0)),
                      expert((D_MODEL, D_EXPERT)), expert((D_MODEL, D_EXPERT)),
                      expert((D_EXPERT, D_MODEL))],
            out_specs=pl.BlockSpec((FFN_TM, D_MODEL), lambda i, te, nt: (i, 0))),
        out_shape=jax.ShapeDtypeStruct((xs.shape[0], D_MODEL), BF16),
        compiler_params=pltpu.CompilerParams(
            dimension_semantics=("arbitrary",), vmem_limit_bytes=VMEM_LIMIT),
        name="moe_expert_ffn",
    )(plan["tile_expert"], plan["n_tiles"], xs, wg, wu, wd)


def _combine_body(loc_ref, dest_ref, npieces_ref, cols_ref, x1_ref, gfin_ref, ys_hbm, o_ref,
                  buf_ref, sem, *, final_norm):
    b = pl.program_id(0)
    slot = b % 2

    def piece_copy(k, block, s):
        src = ys_hbm.at[pl.ds(pl.multiple_of(dest_ref[block * BLOCK_PIECES + k], PIECE), PIECE), :]
        dst = buf_ref.at[s, pl.ds(pl.multiple_of(k * PIECE, PIECE), PIECE), :]
        return pltpu.make_async_copy(src, dst, sem.at[s])

    @pl.when(b == 0)
    def _first_block():
        buf_ref[...] = jnp.zeros_like(buf_ref)
        _for_each(npieces_ref[0], lambda k: piece_copy(k, 0, 0).start())

    @pl.when(b + 1 < pl.num_programs(0))
    def _prefetch_next_block():
        _for_each(npieces_ref[b + 1], lambda k: piece_copy(k, b + 1, 1 - slot).start())

    _for_each(npieces_ref[b], lambda k: piece_copy(k, b, slot).wait())

    idx1 = _block_row(cols_ref[:, 2:3], cols_ref[:, 4:5], loc_ref, b)
    idx2 = _block_row(cols_ref[:, 3:4], cols_ref[:, 5:6], loc_ref, b)
    col = lax.broadcasted_iota(jnp.int32, (MERGE_TM, BLOCK_ROWS), 1).astype(F32)
    unpermute = jnp.where((col == idx1) | (col == idx2), 1.0, 0.0).astype(BF16)
    out = x1_ref[...] + _dot(unpermute, buf_ref[slot])
    o_ref[...] = _rms(out, gfin_ref[...]) if final_norm else out


def _combine(plan, route, x1, ys, g_final, final_norm):
    n = x1.shape[0]
    return pl.pallas_call(
        functools.partial(_combine_body, final_norm=final_norm),
        grid_spec=pltpu.PrefetchScalarGridSpec(
            num_scalar_prefetch=3,
            grid=(n // MERGE_TM,),
            in_specs=[pl.BlockSpec((MERGE_TM, ROUTE_W), lambda i, *_: (i, 0)),
                      pl.BlockSpec((MERGE_TM, D_MODEL), lambda i, *_: (i, 0)),
                      pl.BlockSpec((1, D_MODEL), lambda i, *_: (0, 0)),
                      pl.BlockSpec(memory_space=pl.ANY)],
            out_specs=pl.BlockSpec((MERGE_TM, D_MODEL), lambda i, *_: (i, 0)),
            scratch_shapes=[pltpu.VMEM((2, BLOCK_ROWS, D_MODEL), BF16),
                            pltpu.SemaphoreType.DMA((2,))]),
        out_shape=jax.ShapeDtypeStruct((n, D_MODEL), F32),
        compiler_params=pltpu.CompilerParams(
            dimension_semantics=("arbitrary",), vmem_limit_bytes=VMEM_LIMIT),
        name="moe_combine",
    )(plan["loc"], plan["dest"], plan["n_pieces"], route, x1, g_final.reshape(1, D_MODEL), ys)


def kernel(x, mem, g_mix, w_in, da_lambda, da_sub_g, hg_lower_bounds, hg_out_g, g_mem, w_mem_kv, w_branch, w_out, g_ffn, w_router, router_bias, w_exp_gate, w_exp_up, w_exp_down, g_final):
    b, s, d = x.shape
    m = mem.shape[1]
    n = b * s
    wr_pad = jnp.pad(w_router.astype(F32), ((0, 0), (0, ROUTE_W - N_EXPERTS)))
    wr_hi, wr_lo = _split_bf16(wr_pad)
    x2d = x.reshape(n, d)
    mem2d = mem.reshape(b * m, d)
    for l in range(DEPTH):
        lam_init = 0.8 - 0.6 * math.exp(-0.3 * l)
        proj = _norm_proj(x2d, g_mix[l], w_in[l].astype(BF16), row_tile=512, name="in_proj")
        proj3 = proj.reshape(b, s, IN_TOTAL)
        mkv = _norm_proj(mem2d, g_mem[l], w_mem_kv[l].astype(BF16), row_tile=512,
                         name="mem_kv_proj")
        y_a = _diff_attention(proj3, da_lambda[l].astype(F32), da_sub_g[l], lam_init)
        y_b = _hgrn2(proj3, hg_lower_bounds.astype(F32), hg_out_g[l], l)
        y_c = _cross_attention(proj3, mkv.reshape(b, m, 2 * MIX_W))
        x1, h2, route, route_rows, counts = _merge(
            y_a.reshape(n, MIX_W), y_b.reshape(n, MIX_W), y_c.reshape(n, MIX_W), proj, x2d,
            w_branch[l].astype(BF16), w_out[l].astype(BF16), g_ffn[l],
            wr_hi, wr_lo, router_bias.astype(F32))
        plan = _dispatch_plan(counts)
        xs = _dispatch(plan, route_rows, route, h2)
        ys = _expert_ffn(plan, xs, w_exp_gate[l].astype(BF16), w_exp_up[l].astype(BF16),
                         w_exp_down[l].astype(BF16))
        x2d = _combine(plan, route, x1, ys, g_final, final_norm=(l == DEPTH - 1))
    return x2d.reshape(b, s, d)
```

```python
import functools
import math

import numpy as np
import jax
import jax.numpy as jnp
from jax import lax
from jax.experimental import pallas as pl
from jax.experimental.pallas import tpu as pltpu

F32 = jnp.float32
BF16 = jnp.bfloat16

D_MODEL = 1024
DEPTH = 2
DA_HEADS = 4
DA_QK_DIM = 64
HEAD_W = 128
MIX_W = 512
HG_HEADS = 4
HG_CHUNK = 64
CA_HEADS = 4
N_BRANCH = 3
IN_TOTAL = 8 * MIX_W + N_BRANCH * D_MODEL
N_EXPERTS = 16
N_GROUPS = 4
EXPERTS_PER_GROUP = 4
D_EXPERT = 512
EPS = 1e-6

COL_DA_Q, COL_DA_K, COL_DA_V, COL_HG_Q, COL_HG_F, COL_HG_I, COL_HG_G, COL_CA_Q = range(8)
COL_GATES = 8 * MIX_W

VMEM_LIMIT = 56 * 1024 * 1024

NEG_INF = float("-inf")


def _dot(a, b):
    return jnp.dot(a, b, preferred_element_type=F32)


def _dot_nt(a, b):
    return lax.dot_general(a, b, (((1,), (1,)), ((), ())), preferred_element_type=F32)


def _dot_tn(a, b):
    return lax.dot_general(a, b, (((0,), (0,)), ((), ())), preferred_element_type=F32)


def _rms(x, g):
    return x * lax.rsqrt(jnp.mean(x * x, axis=-1, keepdims=True) + EPS) * g


def _split_bf16(x):
    hi = x.astype(BF16)
    lo = (x - hi.astype(F32)).astype(BF16)
    return hi, lo


def _norm_proj_body(x_ref, g_ref, w_ref, o_ref, *, col_chunk):
    h = _rms(x_ref[...], g_ref[...]).astype(BF16)
    for c in range(o_ref.shape[1] // col_chunk):
        cs = slice(c * col_chunk, (c + 1) * col_chunk)
        o_ref[:, cs] = _dot(h, w_ref[:, cs]).astype(BF16)


def _norm_proj(x2d, g, w_bf16, *, row_tile, name):
    n, d = x2d.shape
    width = w_bf16.shape[1]
    return pl.pallas_call(
        functools.partial(_norm_proj_body, col_chunk=512),
        grid=(n // row_tile,),
        in_specs=[
            pl.BlockSpec((row_tile, d), lambda i: (i, 0)),
            pl.BlockSpec((1, d), lambda i: (0, 0)),
            pl.BlockSpec((d, width), lambda i: (0, 0), pipeline_mode=pl.Buffered(1)),
        ],
        out_specs=pl.BlockSpec((row_tile, width), lambda i: (i, 0)),
        out_shape=jax.ShapeDtypeStruct((n, width), BF16),
        compiler_params=pltpu.CompilerParams(
            dimension_semantics=("parallel",), vmem_limit_bytes=VMEM_LIMIT),
        name=name,
    )(x2d, g.reshape(1, d), w_bf16)


DA_TQ = 512
DA_TK = 256
DA_VROWS = HEAD_W + 16
LOG2E = math.log2(math.e)
DA_QSCALE = DA_QK_DIM ** -0.5 * LOG2E
ALIBI_SLOPES = tuple(2.0 ** (-8.0 * (i + 1) / DA_HEADS) for i in range(DA_HEADS))
assert all(math.frexp(s)[0] == 0.5 for s in ALIBI_SLOPES) and DA_TK <= 256


def _bf16_terms(x, n):
    terms, rest = [], np.float32(x)
    for _ in range(n):
        t = rest.astype(jnp.bfloat16)
        terms.append(float(t))
        rest = np.float32(rest - np.float32(t))
    return tuple(terms)


LOG2E_TERMS = _bf16_terms(LOG2E, 3)


def _own_half(idx, m):
    return idx < DA_QK_DIM if m == 0 else idx >= DA_QK_DIM


def _bias_base(m):
    return DA_QK_DIM * (1 - m)


def _diff_attn_body(q_ref, k_ref, v_ref, lam_ref, subg_ref, o_ref, k0_ref, k1_ref, vt_ref,
                    s_ref, max_ref, acc_ref, *, lam_init):
    h = pl.program_id(1)
    qi = pl.program_id(2)
    seq = k_ref.shape[0]
    slope = jnp.where(h == 0, ALIBI_SLOPES[0],
                      jnp.where(h == 1, ALIBI_SLOPES[1],
                                jnp.where(h == 2, ALIBI_SLOPES[2], ALIBI_SLOPES[3]))).astype(F32)

    @pl.when(qi == 0)
    def _prepare_keys_and_values():
        k = k_ref[...].astype(F32)
        lane = lax.broadcasted_iota(jnp.int32, (seq, HEAD_W), 1)
        row = lax.broadcasted_iota(jnp.int32, (seq, HEAD_W), 0)
        local_bias = (row & (DA_TK - 1)).astype(F32) * slope
        for m, ka_ref in ((0, k0_ref), (1, k1_ref)):
            base = _bias_base(m)
            slot = (lane >= base) & (lane < base + len(LOG2E_TERMS))
            ka_ref[...] = jnp.where(_own_half(lane, m), k,
                                    jnp.where(slot, local_bias, 0.0)).astype(BF16)
        for j in range(seq // DA_TK):
            vt_ref[j, 0:HEAD_W, :] = (
                v_ref[j * DA_TK:(j + 1) * DA_TK, :].astype(F32).T.astype(BF16))
            vt_ref[j, HEAD_W:DA_VROWS, :] = jnp.ones((DA_VROWS - HEAD_W, DA_TK), BF16)

    qt = (q_ref[...].astype(F32) * DA_QSCALE).T
    feat = lax.broadcasted_iota(jnp.int32, (HEAD_W, DA_TQ), 0)

    def q_aug(m):
        side = jnp.zeros((HEAD_W, DA_TQ), F32)
        for i, c in enumerate(LOG2E_TERMS):
            side = jnp.where(feat == _bias_base(m) + i, c, side)
        return jnp.where(_own_half(feat, m), qt, side).astype(BF16)

    q0t, q1t = q_aug(0), q_aug(1)

    def start_scores(j, slot):
        ks = pl.ds(pl.multiple_of(j * DA_TK, DA_TK), DA_TK)
        s_ref[slot, 0] = _dot(k0_ref[ks, :], q0t)
        s_ref[slot, 1] = _dot(k1_ref[ks, :], q1t)

    def consume(j, slot, mask):
        vt = vt_ref[j]
        tile_bias = (slope * LOG2E) * (j * DA_TK).astype(F32)
        for m in range(2):
            s = s_ref[slot, m]
            if mask is not None:
                s = jnp.where(mask, s, NEG_INF)
            m_old = max_ref[m]
            m_new = jnp.maximum(m_old, jnp.max(s, axis=0, keepdims=True) + tile_bias)
            alpha = jnp.exp2(m_old - m_new)
            p = jnp.exp2(s - (m_new - tile_bias))
            max_ref[m] = m_new
            acc_ref[m] = alpha * acc_ref[m] + _dot(vt, p.astype(BF16))

    max_ref[...] = jnp.full(max_ref.shape, NEG_INF, F32)
    acc_ref[...] = jnp.zeros_like(acc_ref)

    tiles_per_q = DA_TQ // DA_TK
    assert tiles_per_q == 2
    n_full = qi * tiles_per_q
    start_scores(0, 0)

    def full_tile_pair(jj, _):
        j = jj * 2
        start_scores(j + 1, 1)
        consume(j, 0, None)
        start_scores(j + 2, 0)
        consume(j + 1, 1, None)
        return 0

    lax.fori_loop(0, qi, full_tile_pair, 0)
    rk = lax.broadcasted_iota(jnp.int32, (DA_TK, DA_TQ), 0)
    cq = lax.broadcasted_iota(jnp.int32, (DA_TK, DA_TQ), 1)
    start_scores(n_full + 1, 1)
    consume(n_full, 0, cq >= rk)
    consume(n_full + 1, 1, cq >= rk + DA_TK)

    lp = lam_ref[...]
    lam = (jnp.exp(jnp.sum(lp[0:1] * lp[1:2], axis=1, keepdims=True))
           - jnp.exp(jnp.sum(lp[2:3] * lp[3:4], axis=1, keepdims=True)) + lam_init)
    a0, l0 = acc_ref[0, 0:HEAD_W, :], acc_ref[0, HEAD_W:HEAD_W + 1, :]
    a1, l1 = acc_ref[1, 0:HEAD_W, :], acc_ref[1, HEAD_W:HEAD_W + 1, :]
    o_t = a0 / l0 - lam * (a1 / l1)
    y_t = o_t * lax.rsqrt(jnp.mean(o_t * o_t, axis=0, keepdims=True) + EPS) * subg_ref[...]
    o_ref[...] = (y_t * (1.0 - lam_init)).T.astype(BF16)


def _diff_attention(proj3, lam_params, sub_g, lam_init):
    b, s, _ = proj3.shape
    return pl.pallas_call(
        functools.partial(_diff_attn_body, lam_init=lam_init),
        grid=(b, DA_HEADS, s // DA_TQ),
        in_specs=[
            pl.BlockSpec((None, DA_TQ, HEAD_W), lambda bi, h, i: (bi, i, COL_DA_Q * 4 + h)),
            pl.BlockSpec((None, s, HEAD_W), lambda bi, h, i: (bi, 0, COL_DA_K * 4 + h)),
            pl.BlockSpec((None, s, HEAD_W), lambda bi, h, i: (bi, 0, COL_DA_V * 4 + h)),
            pl.BlockSpec((4, DA_QK_DIM), lambda bi, h, i: (0, 0)),
            pl.BlockSpec((HEAD_W, 1), lambda bi, h, i: (0, 0)),
        ],
        out_specs=pl.BlockSpec((None, DA_TQ, HEAD_W), lambda bi, h, i: (bi, i, h)),
        out_shape=jax.ShapeDtypeStruct((b, s, MIX_W), BF16),
        scratch_shapes=[pltpu.VMEM((s, HEAD_W), BF16), pltpu.VMEM((s, HEAD_W), BF16),
                        pltpu.VMEM((s // DA_TK, DA_VROWS, DA_TK), BF16),
                        pltpu.VMEM((2, 2, DA_TK, DA_TQ), F32),
                        pltpu.VMEM((2, 1, DA_TQ), F32),
                        pltpu.VMEM((2, DA_VROWS, DA_TQ), F32)],
        compiler_params=pltpu.CompilerParams(
            dimension_semantics=("parallel", "parallel", "arbitrary"),
            vmem_limit_bytes=VMEM_LIMIT),
        name="diff_attention",
    )(proj3, proj3, proj3, lam_params, sub_g.reshape(HEAD_W, 1))


HG_T = 512
HG_LEVELS = int(math.log2(HG_CHUNK))
HG_SUBLANES = 8
HG_FINE_LEVELS = tuple(lev for lev in range(HG_LEVELS)
                       if HG_CHUNK >> (lev + 1) < HG_SUBLANES)


def _hgrn_constants():
    c = HG_CHUNK
    r = np.arange(c)
    sums = np.zeros((1 + len(HG_FINE_LEVELS), c, c), np.float32)
    masks = np.zeros((HG_LEVELS + 1, c, c), np.float32)
    sums[0] = (r[None, :] <= r[:, None])
    for lev in range(HG_LEVELS):
        half = c >> (lev + 1)
        mid = (r // (2 * half)) * (2 * half) + half
        second = (r % (2 * half)) >= half
        rp = r[None, :]
        t_rows = second[:, None] & (rp >= mid[:, None]) & (rp <= r[:, None])
        s_rows = (~second)[:, None] & (rp > r[:, None]) & (rp < mid[:, None])
        if lev in HG_FINE_LEVELS:
            sums[1 + HG_FINE_LEVELS.index(lev)] = t_rows | s_rows
        same_block = (r[:, None] // (2 * half)) == (r[None, :] // (2 * half))
        masks[lev] = same_block & second[:, None] & (~second)[None, :]
    masks[HG_LEVELS] = np.eye(c)
    return sums.reshape(-1, c), masks


def _coarse_level_sums(bcum, half):
    parts = []
    for r0 in range(0, HG_CHUNK, HG_SUBLANES):
        mid = r0 // (2 * half) * (2 * half) + half
        ref = bcum[mid - 1:mid, :]
        rows = bcum[r0:r0 + HG_SUBLANES, :]
        parts.append(rows - ref if r0 % (2 * half) >= half else ref - rows)
    return jnp.concatenate(parts, axis=0)


def _layer_lower_bound(raw, layer):
    rows = [raw[i:i + 1] for i in range(raw.shape[0])]
    top = functools.reduce(jnp.maximum, rows)
    ex = [jnp.exp(r - top) for r in rows]
    tot = functools.reduce(jnp.add, ex)
    sm = [e / tot for e in ex]
    return functools.reduce(jnp.add, sm[:layer + 1]) - sm[0]


def _hgrn_body(q_ref, f_ref, i_ref, g_ref, lb_ref, outg_ref, sums_ref, masks_ref, o_ref,
               state_ref, *, layer):
    @pl.when(pl.program_id(1) == 0)
    def _reset_state():
        state_ref[...] = jnp.zeros_like(state_ref)

    c = HG_CHUNK
    lb = _layer_lower_bound(lb_ref[...], layer)
    sums = sums_ref[...]

    def chunk(ci, _):
        rows = pl.ds(pl.multiple_of(ci * c, c), c)
        f = lb + (1.0 - lb) * jax.nn.sigmoid(f_ref[rows, :].astype(F32))
        log_f = jnp.log(f)
        kk = 1.0 - f
        qq = jax.nn.silu(q_ref[rows, :].astype(F32))
        lf_hi, lf_lo = _split_bf16(log_f)
        fine = _dot(sums, lf_hi) + _dot(sums, lf_lo)
        bcum = fine[0:c, :]
        e_cum = jnp.exp(bcum)
        e_tail = jnp.exp(bcum[c - 1:c, :] - bcum)
        e_level = [jnp.exp(fine[(1 + HG_FINE_LEVELS.index(lev)) * c:
                                (2 + HG_FINE_LEVELS.index(lev)) * c, :])
                   if lev in HG_FINE_LEVELS else jnp.exp(_coarse_level_sums(bcum, c >> (lev + 1)))
                   for lev in range(HG_LEVELS)]
        gate = jax.nn.silu(g_ref[rows, :].astype(F32))
        heads = [slice(h * HEAD_W, (h + 1) * HEAD_W) for h in range(HG_HEADS)]
        attn = []
        for cs in heads:
            a = masks_ref[HG_LEVELS] * _dot_nt(qq[:, cs].astype(BF16), kk[:, cs].astype(BF16))
            for lev in range(HG_LEVELS):
                e = e_level[lev][:, cs]
                a = a + masks_ref[lev] * _dot_nt((qq[:, cs] * e).astype(BF16),
                                                 (kk[:, cs] * e).astype(BF16))
            attn.append(a)
        carried = [_dot_nt((qq[:, cs] * e_cum[:, cs]).astype(BF16), state_ref[h].astype(BF16))
                   for h, cs in enumerate(heads)]
        update = [_dot_tn(i_ref[rows, cs], (kk[:, cs] * e_tail[:, cs]).astype(BF16))
                  for cs in heads]
        for h, cs in enumerate(heads):
            o = _dot(attn[h].astype(BF16), i_ref[rows, cs]) + carried[h]
            state_ref[h] = state_ref[h] * e_cum[c - 1:c, cs] + update[h]
            o_ref[rows, cs] = (_rms(o, outg_ref[...]) * gate[:, cs]).astype(BF16)
        return 0

    lax.fori_loop(0, HG_T // c, chunk, 0, unroll=2)


def _hgrn2(proj3, lower_bounds, out_g, layer):
    b, s, _ = proj3.shape
    sums, masks = _hgrn_constants()

    def col(cblk):
        return pl.BlockSpec((None, HG_T, MIX_W), lambda bi, t: (bi, t, cblk))

    def whole(shape):
        return pl.BlockSpec(shape, lambda bi, t: (0,) * len(shape))

    return pl.pallas_call(
        functools.partial(_hgrn_body, layer=layer),
        grid=(b, s // HG_T),
        in_specs=[col(COL_HG_Q), col(COL_HG_F), col(COL_HG_I), col(COL_HG_G),
                  whole(lower_bounds.shape), whole((1, HEAD_W)),
                  whole(sums.shape), whole(masks.shape)],
        out_specs=pl.BlockSpec((None, HG_T, MIX_W), lambda bi, t: (bi, t, 0)),
        out_shape=jax.ShapeDtypeStruct((b, s, MIX_W), BF16),
        scratch_shapes=[pltpu.VMEM((HG_HEADS, HEAD_W, HEAD_W), F32)],
        compiler_params=pltpu.CompilerParams(
            dimension_semantics=("parallel", "arbitrary"), vmem_limit_bytes=VMEM_LIMIT),
        name="hgrn2",
    )(proj3, proj3, proj3, proj3, lower_bounds, out_g.reshape(1, HEAD_W),
      jnp.asarray(sums, BF16), jnp.asarray(masks, F32))


CA_TQ = 512


def _cross_attn_body(q_ref, mk_ref, mv_ref, o_ref):
    for h in range(CA_HEADS):
        cs = slice(h * HEAD_W, (h + 1) * HEAD_W)
        s = _dot_nt(q_ref[:, cs], mk_ref[:, cs]) * (HEAD_W ** -0.5)
        p = jnp.exp(s - jnp.max(s, axis=1, keepdims=True))
        l = jnp.sum(p, axis=1, keepdims=True)
        o_ref[:, cs] = (_dot(p.astype(BF16), mv_ref[:, cs]) / l).astype(BF16)


def _cross_attention(proj3, mkv3):
    b, s, _ = proj3.shape
    m = mkv3.shape[1]
    return pl.pallas_call(
        _cross_attn_body,
        grid=(b, s // CA_TQ),
        in_specs=[
            pl.BlockSpec((None, CA_TQ, MIX_W), lambda bi, i: (bi, i, COL_CA_Q)),
            pl.BlockSpec((None, m, MIX_W), lambda bi, i: (bi, 0, 0)),
            pl.BlockSpec((None, m, MIX_W), lambda bi, i: (bi, 0, 1)),
        ],
        out_specs=pl.BlockSpec((None, CA_TQ, MIX_W), lambda bi, i: (bi, i, 0)),
        out_shape=jax.ShapeDtypeStruct((b, s, MIX_W), BF16),
        compiler_params=pltpu.CompilerParams(
            dimension_semantics=("parallel", "parallel"), vmem_limit_bytes=VMEM_LIMIT),
        name="cross_attention",
    )(proj3, mkv3, mkv3)


MERGE_TM = 512
ROUTE_W = 128


def _top2_of_4(r0, r1, r2, r3):
    hi1, lo1 = jnp.maximum(r0, r1), jnp.minimum(r0, r1)
    hi2, lo2 = jnp.maximum(r2, r3), jnp.minimum(r2, r3)
    return jnp.maximum(hi1, hi2), jnp.maximum(jnp.minimum(hi1, hi2), jnp.maximum(lo1, lo2))


def _argmax_first(vals):
    best_v, best_i = vals[0], jnp.zeros_like(vals[0])
    for i in range(1, len(vals)):
        upd = vals[i] > best_v
        best_i = jnp.where(upd, float(i), best_i)
        best_v = jnp.where(upd, vals[i], best_v)
    return best_i


def _pick(idx, vals):
    out = vals[0]
    for i in range(1, len(vals)):
        out = jnp.where(idx == float(i), vals[i], out)
    return out


def _route(logits_t, bias_col):
    scores = jax.nn.sigmoid(logits_t)
    sel = scores + bias_col
    sel_rows = [sel[e:e + 1, :] for e in range(N_EXPERTS)]
    score_rows = [scores[e:e + 1, :] for e in range(N_EXPERTS)]
    grp = []
    for g in range(N_GROUPS):
        m1, m2 = _top2_of_4(*sel_rows[4 * g:4 * g + 4])
        grp.append(m1 + m2)
    best = _argmax_first(grp)
    cand = [_pick(best, [sel_rows[4 * g + i] for g in range(N_GROUPS)])
            for i in range(EXPERTS_PER_GROUP)]
    cand_score = [_pick(best, [score_rows[4 * g + i] for g in range(N_GROUPS)])
                  for i in range(EXPERTS_PER_GROUP)]
    i1 = _argmax_first(cand)
    i2 = _argmax_first([jnp.where(i1 == float(i), NEG_INF, cand[i])
                        for i in range(EXPERTS_PER_GROUP)])
    w1, w2 = _pick(i1, cand_score), _pick(i2, cand_score)
    tot = w1 + w2
    return w1 / tot, w2 / tot, best * 4.0 + i1, best * 4.0 + i2


def _merge_body(ya_ref, yb_ref, yc_ref, ga_ref, gb_ref, gc_ref, x_ref, wb_ref, wo_ref,
                gffn_ref, wr_hi_ref, wr_lo_ref, rbias_ref, tri_ref,
                x1_ref, h2_ref, route_ref, route_rows_ref, counts_ref):
    halves = [slice(0, MERGE_TM // 2), slice(MERGE_TM // 2, MERGE_TM)]
    mixers = ((ya_ref, ga_ref), (yb_ref, gb_ref), (yc_ref, gc_ref))
    branch = [[_dot(y_ref[rows, :], wb_ref[i]) for i, (y_ref, _) in enumerate(mixers)]
              for rows in halves]
    merged = [sum(jax.nn.sigmoid(gate_ref[rows, :].astype(F32)) * branch[hf][i]
                  for i, (_, gate_ref) in enumerate(mixers))
              for hf, rows in enumerate(halves)]
    x1 = [x_ref[rows, :] + _dot(merged[hf].astype(BF16), wo_ref[...])
          for hf, rows in enumerate(halves)]
    logits = []
    for hf, rows in enumerate(halves):
        x1_ref[rows, :] = x1[hf]
        h2 = _rms(x1[hf], gffn_ref[...])
        h2_ref[rows, :] = h2.astype(BF16)
        h_hi, h_lo = _split_bf16(h2)
        logits.append(_dot(h_hi, wr_hi_ref[...])
                      + (_dot(h_hi, wr_lo_ref[...]) + _dot(h_lo, wr_hi_ref[...])))
    logits_t = jnp.concatenate(logits, axis=0).T[0:N_EXPERTS, :]
    w1, w2, e1, e2 = _route(logits_t, rbias_ref[...])
    expert = lax.broadcasted_iota(jnp.int32, (N_EXPERTS, MERGE_TM), 0).astype(F32)
    pick1 = jnp.where(expert == e1, 1.0, 0.0)
    pick2 = jnp.where(expert == e2, 1.0, 0.0)
    picked = pick1 + pick2
    before = _dot(picked.astype(BF16), tri_ref[...])
    r1 = jnp.sum(pick1 * before, axis=0, keepdims=True)
    r2 = jnp.sum(pick2 * before, axis=0, keepdims=True)
    counts_ref[...] = jnp.broadcast_to(jnp.sum(picked, axis=1, keepdims=True),
                                       counts_ref.shape)
    fields = (w1, w2, e1, e2, r1, r2)
    row = lax.broadcasted_iota(jnp.int32, (ROUTE_W, MERGE_TM), 0)
    packed = jnp.zeros((ROUTE_W, MERGE_TM), F32)
    for i, f in enumerate(fields):
        packed = jnp.where(row == i, f, packed)
    route_ref[...] = packed.T
    route_rows_ref[...] = packed[0:8, :]


def _merge(ya, yb, yc, proj, x2d, w_branch, w_out, g_ffn, wr_hi, wr_lo, rbias):
    n = x2d.shape[0]
    tm = MERGE_TM

    def rows(width, cblk=0):
        return pl.BlockSpec((tm, width), lambda i: (i, cblk))

    def whole(shape):
        return pl.BlockSpec(shape, lambda i: (0,) * len(shape), pipeline_mode=pl.Buffered(1))

    return pl.pallas_call(
        _merge_body,
        grid=(n // tm,),
        in_specs=[rows(MIX_W), rows(MIX_W), rows(MIX_W),
                  rows(D_MODEL, COL_GATES // D_MODEL), rows(D_MODEL, COL_GATES // D_MODEL + 1),
                  rows(D_MODEL, COL_GATES // D_MODEL + 2),
                  rows(D_MODEL),
                  whole(w_branch.shape), whole(w_out.shape), whole((1, D_MODEL)),
                  whole(wr_hi.shape), whole(wr_lo.shape), whole((N_EXPERTS, 1)),
                  whole((tm, tm))],
        out_specs=[rows(D_MODEL), rows(D_MODEL), rows(ROUTE_W),
                   pl.BlockSpec((None, 8, tm), lambda i: (i, 0, 0)),
                   pl.BlockSpec((None, N_EXPERTS, ROUTE_W), lambda i: (i, 0, 0))],
        out_shape=[jax.ShapeDtypeStruct((n, D_MODEL), F32),
                   jax.ShapeDtypeStruct((n, D_MODEL), BF16),
                   jax.ShapeDtypeStruct((n, ROUTE_W), F32),
                   jax.ShapeDtypeStruct((n // tm, 8, tm), F32),
                   jax.ShapeDtypeStruct((n // tm, N_EXPERTS, ROUTE_W), F32)],
        compiler_params=pltpu.CompilerParams(
            dimension_semantics=("parallel",), vmem_limit_bytes=VMEM_LIMIT),
        name="merge_route",
    )(ya, yb, yc, proj, proj, proj, x2d, w_branch, w_out, g_ffn.reshape(1, D_MODEL), wr_hi, wr_lo,
      rbias.reshape(N_EXPERTS, 1),
      jnp.asarray(np.triu(np.ones((tm, tm), np.float32), 1), BF16))


PIECE = 16
BLOCK_ROWS = -(-(2 * MERGE_TM + N_EXPERTS * (PIECE - 1)) // 256) * 256
BLOCK_PIECES = BLOCK_ROWS // PIECE
FFN_TM = 512
XS_W = D_MODEL + 128
TAIL_PIECES = FFN_TM // PIECE


def _max_ffn_tiles(n_blocks):
    rows = n_blocks * (2 * MERGE_TM + N_EXPERTS * (PIECE - 1)) + N_EXPERTS * (FFN_TM - PIECE)
    return -(-rows // FFN_TM)


def _dispatch_plan(counts_out):
    i32 = jnp.int32
    counts = counts_out[:, :, 0].astype(i32)
    n_blocks = counts.shape[0]
    padded = (counts + PIECE - 1) // PIECE * PIECE
    loc = jnp.cumsum(padded, axis=1) - padded
    tot = jnp.sum(padded, axis=0)
    region = (tot + FFN_TM - 1) // FFN_TM * FFN_TM
    off = jnp.cumsum(region) - region
    seg = off[None, :] + jnp.cumsum(padded, axis=0) - padded
    experts = jnp.arange(N_EXPERTS, dtype=i32)

    def pick(table, e_idx):
        return jnp.sum(jnp.where(e_idx[..., None] == experts, table, 0), axis=-1, dtype=i32)

    k_row = jnp.arange(BLOCK_PIECES, dtype=i32) * PIECE
    e_of = jnp.sum(k_row[None, :, None] >= (loc + padded)[:, None, :], axis=2, dtype=i32)
    dest = pick((seg - loc)[:, None, :], jnp.minimum(e_of, N_EXPERTS - 1)) + k_row[None, :]
    n_tail = (region - tot) // PIECE
    tail_end = jnp.cumsum(n_tail)
    j = jnp.arange(N_EXPERTS * TAIL_PIECES, dtype=i32)
    e_tail = jnp.minimum(jnp.sum(j[:, None] >= tail_end[None, :], axis=1, dtype=i32),
                         N_EXPERTS - 1)
    zero_dest = pick((off + tot - (tail_end - n_tail) * PIECE)[None, :], e_tail) + j * PIECE
    tile_row = jnp.arange(_max_ffn_tiles(n_blocks), dtype=i32) * FFN_TM
    tile_expert = jnp.sum(tile_row[:, None] >= (off + region)[None, :], axis=1, dtype=i32)
    return dict(
        loc=loc.reshape(-1), dest=dest.reshape(-1),
        n_pieces=(jnp.sum(padded, axis=1) // PIECE).astype(i32),
        zero_dest=zero_dest, n_zero=tail_end[-1:].astype(i32),
        tile_expert=jnp.minimum(tile_expert, N_EXPERTS - 1),
        n_tiles=(jnp.sum(region) // FFN_TM).astype(i32).reshape(1))


def _block_row(expert, rank, loc_ref, block):
    start = jnp.zeros_like(rank)
    for e in range(N_EXPERTS):
        start = jnp.where(expert == float(e), loc_ref[block * N_EXPERTS + e].astype(F32), start)
    return start + rank


def _row_tags(cols):
    lanes = []
    for w_col in (cols[:, 0:1], cols[:, 1:2]):
        hi = w_col.astype(BF16).astype(F32)
        mid = (w_col - hi).astype(BF16).astype(F32)
        lanes += [hi, mid, w_col - hi - mid]
    lanes += [cols[:, 2:3], cols[:, 3:4]]
    lane = lax.broadcasted_iota(jnp.int32, (cols.shape[0], 128), 1)
    tags = jnp.zeros((cols.shape[0], 128), F32)
    for i, v in enumerate(lanes):
        tags = jnp.where(lane == i, v, tags)
    return tags.astype(BF16)


def _for_each(n, fn):
    lax.fori_loop(0, n, lambda k, c: (fn(k), c)[1], 0)


def _dispatch_body(loc_ref, dest_ref, npieces_ref, zdest_ref, nzero_ref, ntiles_ref,
                   rows_ref, cols_ref, h_ref, xs_hbm, buf_ref, zero_ref, sem, zero_sem):
    b = pl.program_id(0)
    slot = b % 2

    def piece_copy(k, block, s):
        src = buf_ref.at[s, pl.ds(pl.multiple_of(k * PIECE, PIECE), PIECE), :]
        dst = xs_hbm.at[pl.ds(pl.multiple_of(dest_ref[block * BLOCK_PIECES + k], PIECE), PIECE), :]
        return pltpu.make_async_copy(src, dst, sem.at[s])

    def zero_copy(j):
        dst = xs_hbm.at[pl.ds(pl.multiple_of(zdest_ref[j], PIECE), PIECE), :]
        return pltpu.make_async_copy(zero_ref.at[pl.ds(0, PIECE), :], dst, zero_sem)

    def zero_tile_copy(t):
        dst = xs_hbm.at[pl.ds(pl.multiple_of(t * FFN_TM, FFN_TM), FFN_TM), :]
        return pltpu.make_async_copy(zero_ref, dst, zero_sem)

    idle_tiles = xs_hbm.shape[0] // FFN_TM - ntiles_ref[0]

    @pl.when(b == 0)
    def _start_zero_fill():
        zero_ref[...] = jnp.zeros_like(zero_ref)
        _for_each(nzero_ref[0], lambda j: zero_copy(j).start())
        _for_each(idle_tiles, lambda t: zero_tile_copy(ntiles_ref[0] + t).start())

    idx1 = _block_row(rows_ref[2:3, :], rows_ref[4:5, :], loc_ref, b)
    idx2 = _block_row(rows_ref[3:4, :], rows_ref[5:6, :], loc_ref, b)
    row = lax.broadcasted_iota(jnp.int32, (BLOCK_ROWS, MERGE_TM), 0).astype(F32)
    permute = jnp.where((row == idx1) | (row == idx2), 1.0, 0.0).astype(BF16)
    buf_ref[slot, :, 0:D_MODEL] = _dot(permute, h_ref[...]).astype(BF16)
    buf_ref[slot, :, D_MODEL:XS_W] = _dot(permute, _row_tags(cols_ref[...])).astype(BF16)
    _for_each(npieces_ref[b], lambda k: piece_copy(k, b, slot).start())

    @pl.when(b > 0)
    def _wait_previous_block():
        _for_each(npieces_ref[b - 1], lambda k: piece_copy(k, b - 1, 1 - slot).wait())

    @pl.when(b == pl.num_programs(0) - 1)
    def _wait_last_block():
        _for_each(npieces_ref[b], lambda k: piece_copy(k, b, slot).wait())

    @pl.when(b == 0)
    def _wait_zero_fill():
        _for_each(nzero_ref[0], lambda j: zero_copy(j).wait())
        _for_each(idle_tiles, lambda t: zero_tile_copy(ntiles_ref[0] + t).wait())


def _dispatch(plan, route_rows, route, h2):
    n_blocks = route_rows.shape[0]
    rows_max = _max_ffn_tiles(n_blocks) * FFN_TM
    return pl.pallas_call(
        _dispatch_body,
        grid_spec=pltpu.PrefetchScalarGridSpec(
            num_scalar_prefetch=6,
            grid=(n_blocks,),
            in_specs=[pl.BlockSpec((None, 8, MERGE_TM), lambda i, *_: (i, 0, 0)),
                      pl.BlockSpec((MERGE_TM, ROUTE_W), lambda i, *_: (i, 0)),
                      pl.BlockSpec((MERGE_TM, D_MODEL), lambda i, *_: (i, 0))],
            out_specs=pl.BlockSpec(memory_space=pl.ANY),
            scratch_shapes=[pltpu.VMEM((2, BLOCK_ROWS, XS_W), BF16),
                            pltpu.VMEM((FFN_TM, XS_W), BF16),
                            pltpu.SemaphoreType.DMA((2,)),
                            pltpu.SemaphoreType.DMA(())]),
        out_shape=jax.ShapeDtypeStruct((rows_max, XS_W), BF16),
        compiler_params=pltpu.CompilerParams(
            dimension_semantics=("arbitrary",), vmem_limit_bytes=VMEM_LIMIT),
        name="moe_dispatch",
    )(plan["loc"], plan["dest"], plan["n_pieces"], plan["zero_dest"], plan["n_zero"],
      plan["n_tiles"], route_rows, route, h2)


def _ffn_body(tile_expert_ref, ntiles_ref, xs_ref, wg_ref, wu_ref, wd_ref, ys_ref):
    @pl.when(pl.program_id(0) >= ntiles_ref[0])
    def _idle_tile():
        ys_ref[...] = jnp.zeros_like(ys_ref)

    @pl.when(pl.program_id(0) < ntiles_ref[0])
    def _run_tile():
        x = xs_ref[:, 0:D_MODEL]
        tags = xs_ref[:, D_MODEL:XS_W].astype(F32)
        first_pick = tags[:, 6:7] == tile_expert_ref[pl.program_id(0)].astype(F32)
        weight = jnp.where(first_pick, tags[:, 0:1] + tags[:, 1:2] + tags[:, 2:3],
                           tags[:, 3:4] + tags[:, 4:5] + tags[:, 5:6])
        a = jax.nn.silu(_dot(x, wg_ref[...])) * _dot(x, wu_ref[...])
        ys_ref[...] = _dot((a * weight).astype(BF16), wd_ref[...]).astype(BF16)


def _expert_ffn(plan, xs, wg, wu, wd):
    n_tiles_max = xs.shape[0] // FFN_TM

    def tile(i, tile_expert, n_tiles):
        return jnp.minimum(i, n_tiles[0] - 1)

    def expert(shape):
        return pl.BlockSpec((None,) + shape, lambda i, te, nt: (te[tile(i, te, nt)], 0, 0))

    return pl.pallas_call(
        _ffn_body,
        grid_spec=pltpu.PrefetchScalarGridSpec(
            num_scalar_prefetch=2,
            grid=(n_tiles_max,),
            in_specs=[pl.BlockSpec((FFN_TM, XS_W), lambda i, te, nt: (tile(i, te, nt), 0)),
                      expert((D_MODEL, D_EXPERT)), expert((D_MODEL, D_EXPERT)),
                      expert((D_EXPERT, D_MODEL))],
            out_specs=pl.BlockSpec((FFN_TM, D_MODEL), lambda i, te, nt: (i, 0))),
        out_shape=jax.ShapeDtypeStruct((xs.shape[0], D_MODEL), BF16),
        compiler_params=pltpu.CompilerParams(
            dimension_semantics=("arbitrary",), vmem_limit_bytes=VMEM_LIMIT),
        name="moe_expert_ffn",
    )(plan["tile_expert"], plan["n_tiles"], xs, wg, wu, wd)


def _combine_body(loc_ref, dest_ref, npieces_ref, cols_ref, x1_ref, gfin_ref, ys_hbm, o_ref,
                  buf_ref, sem, *, final_norm):
    b = pl.program_id(0)
    slot = b % 2

    def piece_copy(k, block, s):
        src = ys_hbm.at[pl.ds(pl.multiple_of(dest_ref[block * BLOCK_PIECES + k], PIECE), PIECE), :]
        dst = buf_ref.at[s, pl.ds(pl.multiple_of(k * PIECE, PIECE), PIECE), :]
        return pltpu.make_async_copy(src, dst, sem.at[s])

    @pl.when(b == 0)
    def _first_block():
        buf_ref[...] = jnp.zeros_like(buf_ref)
        _for_each(npieces_ref[0], lambda k: piece_copy(k, 0, 0).start())

    @pl.when(b + 1 < pl.num_programs(0))
    def _prefetch_next_block():
        _for_each(npieces_ref[b + 1], lambda k: piece_copy(k, b + 1, 1 - slot).start())

    _for_each(npieces_ref[b], lambda k: piece_copy(k, b, slot).wait())

    idx1 = _block_row(cols_ref[:, 2:3], cols_ref[:, 4:5], loc_ref, b)
    idx2 = _block_row(cols_ref[:, 3:4], cols_ref[:, 5:6], loc_ref, b)
    col = lax.broadcasted_iota(jnp.int32, (MERGE_TM, BLOCK_ROWS), 1).astype(F32)
    unpermute = jnp.where((col == idx1) | (col == idx2), 1.0, 0.0).astype(BF16)
    out = x1_ref[...] + _dot(unpermute, buf_ref[slot])
    o_ref[...] = _rms(out, gfin_ref[...]) if final_norm else out


def _combine(plan, route, x1, ys, g_final, final_norm):
    n = x1.shape[0]
    return pl.pallas_call(
        functools.partial(_combine_body, final_norm=final_norm),
        grid_spec=pltpu.PrefetchScalarGridSpec(
            num_scalar_prefetch=3,
            grid=(n // MERGE_TM,),
            in_specs=[pl.BlockSpec((MERGE_TM, ROUTE_W), lambda i, *_: (i, 0)),
                      pl.BlockSpec((MERGE_TM, D_MODEL), lambda i, *_: (i, 0)),
                      pl.BlockSpec((1, D_MODEL), lambda i, *_: (0, 0)),
                      pl.BlockSpec(memory_space=pl.ANY)],
            out_specs=pl.BlockSpec((MERGE_TM, D_MODEL), lambda i, *_: (i, 0)),
            scratch_shapes=[pltpu.VMEM((2, BLOCK_ROWS, D_MODEL), BF16),
                            pltpu.SemaphoreType.DMA((2,))]),
        out_shape=jax.ShapeDtypeStruct((n, D_MODEL), F32),
        compiler_params=pltpu.CompilerParams(
            dimension_semantics=("arbitrary",), vmem_limit_bytes=VMEM_LIMIT),
        name="moe_combine",
    )(plan["loc"], plan["dest"], plan["n_pieces"], route, x1, g_final.reshape(1, D_MODEL), ys)


def kernel(x, mem, g_mix, w_in, da_lambda, da_sub_g, hg_lower_bounds, hg_out_g, g_mem, w_mem_kv, w_branch, w_out, g_ffn, w_router, router_bias, w_exp_gate, w_exp_up, w_exp_down, g_final):
    b, s, d = x.shape
    m = mem.shape[1]
    n = b * s
    wr_pad = jnp.pad(w_router.astype(F32), ((0, 0), (0, ROUTE_W - N_EXPERTS)))
    wr_hi, wr_lo = _split_bf16(wr_pad)
    x2d = x.reshape(n, d)
    mem2d = mem.reshape(b * m, d)
    for l in range(DEPTH):
        lam_init = 0.8 - 0.6 * math.exp(-0.3 * l)
        proj = _norm_proj(x2d, g_mix[l], w_in[l].astype(BF16), row_tile=512, name="in_proj")
        proj3 = proj.reshape(b, s, IN_TOTAL)
        mkv = _norm_proj(mem2d, g_mem[l], w_mem_kv[l].astype(BF16), row_tile=512,
                         name="mem_kv_proj")
        y_a = _diff_attention(proj3, da_lambda[l].astype(F32), da_sub_g[l], lam_init)
        y_b = _hgrn2(proj3, hg_lower_bounds.astype(F32), hg_out_g[l], l)
        y_c = _cross_attention(proj3, mkv.reshape(b, m, 2 * MIX_W))
        x1, h2, route, route_rows, counts = _merge(
            y_a.reshape(n, MIX_W), y_b.reshape(n, MIX_W), y_c.reshape(n, MIX_W), proj, x2d,
            w_branch[l].astype(BF16), w_out[l].astype(BF16), g_ffn[l],
            wr_hi, wr_lo, router_bias.astype(F32))
        plan = _dispatch_plan(counts)
        xs = _dispatch(plan, route_rows, route, h2)
        ys = _expert_ffn(plan, xs, w_exp_gate[l].astype(BF16), w_exp_up[l].astype(BF16),
                         w_exp_down[l].astype(BF16))
        x2d = _combine(plan, route, x1, ys, g_final, final_norm=(l == DEPTH - 1))
    return x2d.reshape(b, s, d)
```

```python
import functools
import math

import numpy as np
import jax
import jax.numpy as jnp
from jax import lax
from jax.experimental import pallas as pl
from jax.experimental.pallas import tpu as pltpu

F32 = jnp.float32
BF16 = jnp.bfloat16

D_MODEL = 1024
DEPTH = 2
DA_HEADS = 4
DA_QK_DIM = 64
HEAD_W = 128
MIX_W = 512
HG_HEADS = 4
HG_CHUNK = 64
CA_HEADS = 4
N_BRANCH = 3
IN_TOTAL = 8 * MIX_W + N_BRANCH * D_MODEL
N_EXPERTS = 16
N_GROUPS = 4
EXPERTS_PER_GROUP = 4
D_EXPERT = 512
EPS = 1e-6

COL_DA_Q, COL_DA_K, COL_DA_V, COL_HG_Q, COL_HG_F, COL_HG_I, COL_HG_G, COL_CA_Q = range(8)
COL_GATES = 8 * MIX_W

VMEM_LIMIT = 56 * 1024 * 1024

NEG_INF = float("-inf")


def _dot(a, b):
    return jnp.dot(a, b, preferred_element_type=F32)


def _dot_nt(a, b):
    return lax.dot_general(a, b, (((1,), (1,)), ((), ())), preferred_element_type=F32)


def _dot_tn(a, b):
    return lax.dot_general(a, b, (((0,), (0,)), ((), ())), preferred_element_type=F32)


def _rms(x, g):
    return x * lax.rsqrt(jnp.mean(x * x, axis=-1, keepdims=True) + EPS) * g


def _split_bf16(x):
    hi = x.astype(BF16)
    lo = (x - hi.astype(F32)).astype(BF16)
    return hi, lo


def _norm_proj_body(x_ref, g_ref, w_ref, o_ref, *, col_chunk):
    h = _rms(x_ref[...], g_ref[...]).astype(BF16)
    for c in range(o_ref.shape[1] // col_chunk):
        cs = slice(c * col_chunk, (c + 1) * col_chunk)
        o_ref[:, cs] = _dot(h, w_ref[:, cs]).astype(BF16)


def _norm_proj(x2d, g, w_bf16, *, row_tile, name):
    n, d = x2d.shape
    width = w_bf16.shape[1]
    return pl.pallas_call(
        functools.partial(_norm_proj_body, col_chunk=512),
        grid=(n // row_tile,),
        in_specs=[
            pl.BlockSpec((row_tile, d), lambda i: (i, 0)),
            pl.BlockSpec((1, d), lambda i: (0, 0)),
            pl.BlockSpec((d, width), lambda i: (0, 0), pipeline_mode=pl.Buffered(1)),
        ],
        out_specs=pl.BlockSpec((row_tile, width), lambda i: (i, 0)),
        out_shape=jax.ShapeDtypeStruct((n, width), BF16),
        compiler_params=pltpu.CompilerParams(
            dimension_semantics=("parallel",), vmem_limit_bytes=VMEM_LIMIT),
        name=name,
    )(x2d, g.reshape(1, d), w_bf16)


DA_TQ = 512
DA_TK = 256
DA_VROWS = HEAD_W + 16
LOG2E = math.log2(math.e)
DA_QSCALE = DA_QK_DIM ** -0.5 * LOG2E
ALIBI_SLOPES = tuple(2.0 ** (-8.0 * (i + 1) / DA_HEADS) for i in range(DA_HEADS))
assert all(math.frexp(s)[0] == 0.5 for s in ALIBI_SLOPES) and DA_TK <= 256


def _bf16_terms(x, n):
    terms, rest = [], np.float32(x)
    for _ in range(n):
        t = rest.astype(jnp.bfloat16)
        terms.append(float(t))
        rest = np.float32(rest - np.float32(t))
    return tuple(terms)


LOG2E_TERMS = _bf16_terms(LOG2E, 3)


def _own_half(idx, m):
    return idx < DA_QK_DIM if m == 0 else idx >= DA_QK_DIM


def _bias_base(m):
    return DA_QK_DIM * (1 - m)


def _diff_attn_body(q_ref, k_ref, v_ref, lam_ref, subg_ref, o_ref, k0_ref, k1_ref, vt_ref,
                    s_ref, max_ref, acc_ref, *, lam_init):
    h = pl.program_id(1)
    qi = pl.program_id(2)
    seq = k_ref.shape[0]
    slope = jnp.where(h == 0, ALIBI_SLOPES[0],
                      jnp.where(h == 1, ALIBI_SLOPES[1],
                                jnp.where(h == 2, ALIBI_SLOPES[2], ALIBI_SLOPES[3]))).astype(F32)

    @pl.when(qi == 0)
    def _prepare_keys_and_values():
        k = k_ref[...].astype(F32)
        lane = lax.broadcasted_iota(jnp.int32, (seq, HEAD_W), 1)
        row = lax.broadcasted_iota(jnp.int32, (seq, HEAD_W), 0)
        local_bias = (row & (DA_TK - 1)).astype(F32) * slope
        for m, ka_ref in ((0, k0_ref), (1, k1_ref)):
            base = _bias_base(m)
            slot = (lane >= base) & (lane < base + len(LOG2E_TERMS))
            ka_ref[...] = jnp.where(_own_half(lane, m), k,
                                    jnp.where(slot, local_bias, 0.0)).astype(BF16)
        for j in range(seq // DA_TK):
            vt_ref[j, 0:HEAD_W, :] = (
                v_ref[j * DA_TK:(j + 1) * DA_TK, :].astype(F32).T.astype(BF16))
            vt_ref[j, HEAD_W:DA_VROWS, :] = jnp.ones((DA_VROWS - HEAD_W, DA_TK), BF16)

    qt = (q_ref[...].astype(F32) * DA_QSCALE).T
    feat = lax.broadcasted_iota(jnp.int32, (HEAD_W, DA_TQ), 0)

    def q_aug(m):
        side = jnp.zeros((HEAD_W, DA_TQ), F32)
        for i, c in enumerate(LOG2E_TERMS):
            side = jnp.where(feat == _bias_base(m) + i, c, side)
        return jnp.where(_own_half(feat, m), qt, side).astype(BF16)

    q0t, q1t = q_aug(0), q_aug(1)

    all_queries = slice(0, DA_TQ)

    def start_scores(j, slot, qs=all_queries):
        ks = pl.ds(pl.multiple_of(j * DA_TK, DA_TK), DA_TK)
        s_ref[slot, 0, :, qs] = _dot(k0_ref[ks, :], q0t[:, qs])
        s_ref[slot, 1, :, qs] = _dot(k1_ref[ks, :], q1t[:, qs])

    def consume(j, slot, mask, qs=all_queries):
        vt = vt_ref[j]
        tile_bias = (slope * LOG2E) * (j * DA_TK).astype(F32)
        for m in range(2):
            s = s_ref[slot, m, :, qs]
            if mask is not None:
                s = jnp.where(mask, s, NEG_INF)
            m_old = max_ref[m, :, qs]
            m_new = jnp.maximum(m_old, jnp.max(s, axis=0, keepdims=True) + tile_bias)
            alpha = jnp.exp2(m_old - m_new)
            p = jnp.exp2(s - (m_new - tile_bias))
            max_ref[m, :, qs] = m_new
            acc_ref[m, :, qs] = alpha * acc_ref[m, :, qs] + _dot(vt, p.astype(BF16))

    max_ref[...] = jnp.full(max_ref.shape, NEG_INF, F32)
    acc_ref[...] = jnp.zeros_like(acc_ref)

    tiles_per_q = DA_TQ // DA_TK
    assert tiles_per_q == 2
    n_full = qi * tiles_per_q
    start_scores(0, 0)

    def full_tile_pair(jj, _):
        j = jj * 2
        start_scores(j + 1, 1)
        consume(j, 0, None)
        start_scores(j + 2, 0)
        consume(j + 1, 1, None)
        return 0

    lax.fori_loop(0, qi, full_tile_pair, 0)
    late_queries = slice(DA_TK, DA_TQ)
    rk = lax.broadcasted_iota(jnp.int32, (DA_TK, DA_TQ), 0)
    cq = lax.broadcasted_iota(jnp.int32, (DA_TK, DA_TQ), 1)
    start_scores(n_full + 1, 1, late_queries)
    consume(n_full, 0, cq >= rk)
    consume(n_full + 1, 1, (cq >= rk)[:, 0:DA_TK], late_queries)

    lp = lam_ref[...]
    lam = (jnp.exp(jnp.sum(lp[0:1] * lp[1:2], axis=1, keepdims=True))
           - jnp.exp(jnp.sum(lp[2:3] * lp[3:4], axis=1, keepdims=True)) + lam_init)
    a0, l0 = acc_ref[0, 0:HEAD_W, :], acc_ref[0, HEAD_W:HEAD_W + 1, :]
    a1, l1 = acc_ref[1, 0:HEAD_W, :], acc_ref[1, HEAD_W:HEAD_W + 1, :]
    o_t = a0 / l0 - lam * (a1 / l1)
    y_t = o_t * lax.rsqrt(jnp.mean(o_t * o_t, axis=0, keepdims=True) + EPS) * subg_ref[...]
    o_ref[...] = (y_t * (1.0 - lam_init)).T.astype(BF16)


def _diff_attention(proj3, lam_params, sub_g, lam_init):
    b, s, _ = proj3.shape
    return pl.pallas_call(
        functools.partial(_diff_attn_body, lam_init=lam_init),
        grid=(b, DA_HEADS, s // DA_TQ),
        in_specs=[
            pl.BlockSpec((None, DA_TQ, HEAD_W), lambda bi, h, i: (bi, i, COL_DA_Q * 4 + h)),
            pl.BlockSpec((None, s, HEAD_W), lambda bi, h, i: (bi, 0, COL_DA_K * 4 + h)),
            pl.BlockSpec((None, s, HEAD_W), lambda bi, h, i: (bi, 0, COL_DA_V * 4 + h)),
            pl.BlockSpec((4, DA_QK_DIM), lambda bi, h, i: (0, 0)),
            pl.BlockSpec((HEAD_W, 1), lambda bi, h, i: (0, 0)),
        ],
        out_specs=pl.BlockSpec((None, DA_TQ, HEAD_W), lambda bi, h, i: (bi, i, h)),
        out_shape=jax.ShapeDtypeStruct((b, s, MIX_W), BF16),
        scratch_shapes=[pltpu.VMEM((s, HEAD_W), BF16), pltpu.VMEM((s, HEAD_W), BF16),
                        pltpu.VMEM((s // DA_TK, DA_VROWS, DA_TK), BF16),
                        pltpu.VMEM((2, 2, DA_TK, DA_TQ), F32),
                        pltpu.VMEM((2, 1, DA_TQ), F32),
                        pltpu.VMEM((2, DA_VROWS, DA_TQ), F32)],
        compiler_params=pltpu.CompilerParams(
            dimension_semantics=("parallel", "parallel", "arbitrary"),
            vmem_limit_bytes=VMEM_LIMIT),
        name="diff_attention",
    )(proj3, proj3, proj3, lam_params, sub_g.reshape(HEAD_W, 1))


HG_T = 512
HG_LEVELS = int(math.log2(HG_CHUNK))
HG_SUBLANES = 8
HG_FINE_LEVELS = tuple(lev for lev in range(HG_LEVELS)
                       if HG_CHUNK >> (lev + 1) < HG_SUBLANES)


def _hgrn_constants():
    c = HG_CHUNK
    r = np.arange(c)
    sums = np.zeros((1 + len(HG_FINE_LEVELS), c, c), np.float32)
    masks = np.zeros((HG_LEVELS + 1, c, c), np.float32)
    sums[0] = (r[None, :] <= r[:, None])
    for lev in range(HG_LEVELS):
        half = c >> (lev + 1)
        mid = (r // (2 * half)) * (2 * half) + half
        second = (r % (2 * half)) >= half
        rp = r[None, :]
        t_rows = second[:, None] & (rp >= mid[:, None]) & (rp <= r[:, None])
        s_rows = (~second)[:, None] & (rp > r[:, None]) & (rp < mid[:, None])
        if lev in HG_FINE_LEVELS:
            sums[1 + HG_FINE_LEVELS.index(lev)] = t_rows | s_rows
        same_block = (r[:, None] // (2 * half)) == (r[None, :] // (2 * half))
        masks[lev] = same_block & second[:, None] & (~second)[None, :]
    masks[HG_LEVELS] = np.eye(c)
    return sums.reshape(-1, c), masks


def _coarse_level_sums(bcum, half):
    parts = []
    for r0 in range(0, HG_CHUNK, HG_SUBLANES):
        mid = r0 // (2 * half) * (2 * half) + half
        ref = bcum[mid - 1:mid, :]
        rows = bcum[r0:r0 + HG_SUBLANES, :]
        parts.append(rows - ref if r0 % (2 * half) >= half else ref - rows)
    return jnp.concatenate(parts, axis=0)


def _layer_lower_bound(raw, layer):
    rows = [raw[i:i + 1] for i in range(raw.shape[0])]
    top = functools.reduce(jnp.maximum, rows)
    ex = [jnp.exp(r - top) for r in rows]
    tot = functools.reduce(jnp.add, ex)
    sm = [e / tot for e in ex]
    return functools.reduce(jnp.add, sm[:layer + 1]) - sm[0]


def _hgrn_body(q_ref, f_ref, i_ref, g_ref, lb_ref, outg_ref, sums_ref, masks_ref, o_ref,
               state_ref, *, layer):
    @pl.when(pl.program_id(1) == 0)
    def _reset_state():
        state_ref[...] = jnp.zeros_like(state_ref)

    c = HG_CHUNK
    lb = _layer_lower_bound(lb_ref[...], layer)
    sums = sums_ref[...]

    def chunk(ci, _):
        rows = pl.ds(pl.multiple_of(ci * c, c), c)
        f = lb + (1.0 - lb) * jax.nn.sigmoid(f_ref[rows, :].astype(F32))
        log_f = jnp.log(f)
        kk = 1.0 - f
        qq = jax.nn.silu(q_ref[rows, :].astype(F32))
        lf_hi, lf_lo = _split_bf16(log_f)
        fine = _dot(sums, lf_hi) + _dot(sums, lf_lo)
        bcum = fine[0:c, :]
        e_cum = jnp.exp(bcum)
        e_tail = jnp.exp(bcum[c - 1:c, :] - bcum)
        e_level = [jnp.exp(fine[(1 + HG_FINE_LEVELS.index(lev)) * c:
                                (2 + HG_FINE_LEVELS.index(lev)) * c, :])
                   if lev in HG_FINE_LEVELS else jnp.exp(_coarse_level_sums(bcum, c >> (lev + 1)))
                   for lev in range(HG_LEVELS)]
        gate = jax.nn.silu(g_ref[rows, :].astype(F32))
        heads = [slice(h * HEAD_W, (h + 1) * HEAD_W) for h in range(HG_HEADS)]
        attn = []
        for cs in heads:
            a = masks_ref[HG_LEVELS] * _dot_nt(qq[:, cs].astype(BF16), kk[:, cs].astype(BF16))
            for lev in range(HG_LEVELS):
                e = e_level[lev][:, cs]
                a = a + masks_ref[lev] * _dot_nt((qq[:, cs] * e).astype(BF16),
                                                 (kk[:, cs] * e).astype(BF16))
            attn.append(a)
        carried = [_dot_nt((qq[:, cs] * e_cum[:, cs]).astype(BF16), state_ref[h].astype(BF16))
                   for h, cs in enumerate(heads)]
        update = [_dot_tn(i_ref[rows, cs], (kk[:, cs] * e_tail[:, cs]).astype(BF16))
                  for cs in heads]
        for h, cs in enumerate(heads):
            o = _dot(attn[h].astype(BF16), i_ref[rows, cs]) + carried[h]
            state_ref[h] = state_ref[h] * e_cum[c - 1:c, cs] + update[h]
            o_ref[rows, cs] = (_rms(o, outg_ref[...]) * gate[:, cs]).astype(BF16)
        return 0

    lax.fori_loop(0, HG_T // c, chunk, 0, unroll=2)


def _hgrn2(proj3, lower_bounds, out_g, layer):
    b, s, _ = proj3.shape
    sums, masks = _hgrn_constants()

    def col(cblk):
        return pl.BlockSpec((None, HG_T, MIX_W), lambda bi, t: (bi, t, cblk))

    def whole(shape):
        return pl.BlockSpec(shape, lambda bi, t: (0,) * len(shape))

    return pl.pallas_call(
        functools.partial(_hgrn_body, layer=layer),
        grid=(b, s // HG_T),
        in_specs=[col(COL_HG_Q), col(COL_HG_F), col(COL_HG_I), col(COL_HG_G),
                  whole(lower_bounds.shape), whole((1, HEAD_W)),
                  whole(sums.shape), whole(masks.shape)],
        out_specs=pl.BlockSpec((None, HG_T, MIX_W), lambda bi, t: (bi, t, 0)),
        out_shape=jax.ShapeDtypeStruct((b, s, MIX_W), BF16),
        scratch_shapes=[pltpu.VMEM((HG_HEADS, HEAD_W, HEAD_W), F32)],
        compiler_params=pltpu.CompilerParams(
            dimension_semantics=("parallel", "arbitrary"), vmem_limit_bytes=VMEM_LIMIT),
        name="hgrn2",
    )(proj3, proj3, proj3, proj3, lower_bounds, out_g.reshape(1, HEAD_W),
      jnp.asarray(sums, BF16), jnp.asarray(masks, F32))


CA_TQ = 512


def _cross_attn_body(q_ref, mk_ref, mv_ref, o_ref):
    for h in range(CA_HEADS):
        cs = slice(h * HEAD_W, (h + 1) * HEAD_W)
        s = _dot_nt(q_ref[:, cs], mk_ref[:, cs]) * (HEAD_W ** -0.5)
        p = jnp.exp(s - jnp.max(s, axis=1, keepdims=True))
        l = jnp.sum(p, axis=1, keepdims=True)
        o_ref[:, cs] = (_dot(p.astype(BF16), mv_ref[:, cs]) / l).astype(BF16)


def _cross_attention(proj3, mkv3):
    b, s, _ = proj3.shape
    m = mkv3.shape[1]
    return pl.pallas_call(
        _cross_attn_body,
        grid=(b, s // CA_TQ),
        in_specs=[
            pl.BlockSpec((None, CA_TQ, MIX_W), lambda bi, i: (bi, i, COL_CA_Q)),
            pl.BlockSpec((None, m, MIX_W), lambda bi, i: (bi, 0, 0)),
            pl.BlockSpec((None, m, MIX_W), lambda bi, i: (bi, 0, 1)),
        ],
        out_specs=pl.BlockSpec((None, CA_TQ, MIX_W), lambda bi, i: (bi, i, 0)),
        out_shape=jax.ShapeDtypeStruct((b, s, MIX_W), BF16),
        compiler_params=pltpu.CompilerParams(
            dimension_semantics=("parallel", "parallel"), vmem_limit_bytes=VMEM_LIMIT),
        name="cross_attention",
    )(proj3, mkv3, mkv3)


MERGE_TM = 512
ROUTE_W = 128


def _top2_of_4(r0, r1, r2, r3):
    hi1, lo1 = jnp.maximum(r0, r1), jnp.minimum(r0, r1)
    hi2, lo2 = jnp.maximum(r2, r3), jnp.minimum(r2, r3)
    return jnp.maximum(hi1, hi2), jnp.maximum(jnp.minimum(hi1, hi2), jnp.maximum(lo1, lo2))


def _argmax_first(vals):
    best_v, best_i = vals[0], jnp.zeros_like(vals[0])
    for i in range(1, len(vals)):
        upd = vals[i] > best_v
        best_i = jnp.where(upd, float(i), best_i)
        best_v = jnp.where(upd, vals[i], best_v)
    return best_i


def _pick(idx, vals):
    out = vals[0]
    for i in range(1, len(vals)):
        out = jnp.where(idx == float(i), vals[i], out)
    return out


def _route(logits_t, bias_col):
    scores = jax.nn.sigmoid(logits_t)
    sel = scores + bias_col
    sel_rows = [sel[e:e + 1, :] for e in range(N_EXPERTS)]
    score_rows = [scores[e:e + 1, :] for e in range(N_EXPERTS)]
    grp = []
    for g in range(N_GROUPS):
        m1, m2 = _top2_of_4(*sel_rows[4 * g:4 * g + 4])
        grp.append(m1 + m2)
    best = _argmax_first(grp)
    cand = [_pick(best, [sel_rows[4 * g + i] for g in range(N_GROUPS)])
            for i in range(EXPERTS_PER_GROUP)]
    cand_score = [_pick(best, [score_rows[4 * g + i] for g in range(N_GROUPS)])
                  for i in range(EXPERTS_PER_GROUP)]
    i1 = _argmax_first(cand)
    i2 = _argmax_first([jnp.where(i1 == float(i), NEG_INF, cand[i])
                        for i in range(EXPERTS_PER_GROUP)])
    w1, w2 = _pick(i1, cand_score), _pick(i2, cand_score)
    tot = w1 + w2
    return w1 / tot, w2 / tot, best * 4.0 + i1, best * 4.0 + i2


def _merge_body(ya_ref, yb_ref, yc_ref, ga_ref, gb_ref, gc_ref, x_ref, wb_ref, wo_ref,
                gffn_ref, wr_hi_ref, wr_lo_ref, rbias_ref, tri_ref,
                x1_ref, h2_ref, route_ref, route_rows_ref, counts_ref):
    halves = [slice(0, MERGE_TM // 2), slice(MERGE_TM // 2, MERGE_TM)]
    mixers = ((ya_ref, ga_ref), (yb_ref, gb_ref), (yc_ref, gc_ref))
    branch = [[_dot(y_ref[rows, :], wb_ref[i]) for i, (y_ref, _) in enumerate(mixers)]
              for rows in halves]
    merged = [sum(jax.nn.sigmoid(gate_ref[rows, :].astype(F32)) * branch[hf][i]
                  for i, (_, gate_ref) in enumerate(mixers))
              for hf, rows in enumerate(halves)]
    x1 = [x_ref[rows, :] + _dot(merged[hf].astype(BF16), wo_ref[...])
          for hf, rows in enumerate(halves)]
    logits = []
    for hf, rows in enumerate(halves):
        x1_ref[rows, :] = x1[hf]
        h2 = _rms(x1[hf], gffn_ref[...])
        h2_ref[rows, :] = h2.astype(BF16)
        h_hi, h_lo = _split_bf16(h2)
        logits.append(_dot(h_hi, wr_hi_ref[...])
                      + (_dot(h_hi, wr_lo_ref[...]) + _dot(h_lo, wr_hi_ref[...])))
    logits_t = jnp.concatenate(logits, axis=0).T[0:N_EXPERTS, :]
    w1, w2, e1, e2 = _route(logits_t, rbias_ref[...])
    expert = lax.broadcasted_iota(jnp.int32, (N_EXPERTS, MERGE_TM), 0).astype(F32)
    pick1 = jnp.where(expert == e1, 1.0, 0.0)
    pick2 = jnp.where(expert == e2, 1.0, 0.0)
    picked = pick1 + pick2
    before = _dot(picked.astype(BF16), tri_ref[...])
    r1 = jnp.sum(pick1 * before, axis=0, keepdims=True)
    r2 = jnp.sum(pick2 * before, axis=0, keepdims=True)
    counts_ref[...] = jnp.broadcast_to(jnp.sum(picked, axis=1, keepdims=True),
                                       counts_ref.shape)
    fields = (w1, w2, e1, e2, r1, r2)
    row = lax.broadcasted_iota(jnp.int32, (ROUTE_W, MERGE_TM), 0)
    packed = jnp.zeros((ROUTE_W, MERGE_TM), F32)
    for i, f in enumerate(fields):
        packed = jnp.where(row == i, f, packed)
    route_ref[...] = packed.T
    route_rows_ref[...] = packed[0:8, :]


def _merge(ya, yb, yc, proj, x2d, w_branch, w_out, g_ffn, wr_hi, wr_lo, rbias):
    n = x2d.shape[0]
    tm = MERGE_TM

    def rows(width, cblk=0):
        return pl.BlockSpec((tm, width), lambda i: (i, cblk))

    def whole(shape):
        return pl.BlockSpec(shape, lambda i: (0,) * len(shape), pipeline_mode=pl.Buffered(1))

    return pl.pallas_call(
        _merge_body,
        grid=(n // tm,),
        in_specs=[rows(MIX_W), rows(MIX_W), rows(MIX_W),
                  rows(D_MODEL, COL_GATES // D_MODEL), rows(D_MODEL, COL_GATES // D_MODEL + 1),
                  rows(D_MODEL, COL_GATES // D_MODEL + 2),
                  rows(D_MODEL),
                  whole(w_branch.shape), whole(w_out.shape), whole((1, D_MODEL)),
                  whole(wr_hi.shape), whole(wr_lo.shape), whole((N_EXPERTS, 1)),
                  whole((tm, tm))],
        out_specs=[rows(D_MODEL), rows(D_MODEL), rows(ROUTE_W),
                   pl.BlockSpec((None, 8, tm), lambda i: (i, 0, 0)),
                   pl.BlockSpec((None, N_EXPERTS, ROUTE_W), lambda i: (i, 0, 0))],
        out_shape=[jax.ShapeDtypeStruct((n, D_MODEL), F32),
                   jax.ShapeDtypeStruct((n, D_MODEL), BF16),
                   jax.ShapeDtypeStruct((n, ROUTE_W), F32),
                   jax.ShapeDtypeStruct((n // tm, 8, tm), F32),
                   jax.ShapeDtypeStruct((n // tm, N_EXPERTS, ROUTE_W), F32)],
        compiler_params=pltpu.CompilerParams(
            dimension_semantics=("parallel",), vmem_limit_bytes=VMEM_LIMIT),
        name="merge_route",
    )(ya, yb, yc, proj, proj, proj, x2d, w_branch, w_out, g_ffn.reshape(1, D_MODEL), wr_hi, wr_lo,
      rbias.reshape(N_EXPERTS, 1),
      jnp.asarray(np.triu(np.ones((tm, tm), np.float32), 1), BF16))


PIECE = 16
BLOCK_ROWS = -(-(2 * MERGE_TM + N_EXPERTS * (PIECE - 1)) // 256) * 256
BLOCK_PIECES = BLOCK_ROWS // PIECE
FFN_TM = 512
XS_W = D_MODEL + 128
TAIL_PIECES = FFN_TM // PIECE


def _max_ffn_tiles(n_blocks):
    rows = n_blocks * (2 * MERGE_TM + N_EXPERTS * (PIECE - 1)) + N_EXPERTS * (FFN_TM - PIECE)
    return -(-rows // FFN_TM)


def _dispatch_plan(counts_out):
    i32 = jnp.int32
    counts = counts_out[:, :, 0].astype(i32)
    n_blocks = counts.shape[0]
    padded = (counts + PIECE - 1) // PIECE * PIECE
    loc = jnp.cumsum(padded, axis=1) - padded
    tot = jnp.sum(padded, axis=0)
    region = (tot + FFN_TM - 1) // FFN_TM * FFN_TM
    off = jnp.cumsum(region) - region
    seg = off[None, :] + jnp.cumsum(padded, axis=0) - padded
    experts = jnp.arange(N_EXPERTS, dtype=i32)

    def pick(table, e_idx):
        return jnp.sum(jnp.where(e_idx[..., None] == experts, table, 0), axis=-1, dtype=i32)

    k_row = jnp.arange(BLOCK_PIECES, dtype=i32) * PIECE
    e_of = jnp.sum(k_row[None, :, None] >= (loc + padded)[:, None, :], axis=2, dtype=i32)
    dest = pick((seg - loc)[:, None, :], jnp.minimum(e_of, N_EXPERTS - 1)) + k_row[None, :]
    n_tail = (region - tot) // PIECE
    tail_end = jnp.cumsum(n_tail)
    j = jnp.arange(N_EXPERTS * TAIL_PIECES, dtype=i32)
    e_tail = jnp.minimum(jnp.sum(j[:, None] >= tail_end[None, :], axis=1, dtype=i32),
                         N_EXPERTS - 1)
    zero_dest = pick((off + tot - (tail_end - n_tail) * PIECE)[None, :], e_tail) + j * PIECE
    tile_row = jnp.arange(_max_ffn_tiles(n_blocks), dtype=i32) * FFN_TM
    tile_expert = jnp.sum(tile_row[:, None] >= (off + region)[None, :], axis=1, dtype=i32)
    return dict(
        loc=loc.reshape(-1), dest=dest.reshape(-1),
        n_pieces=(jnp.sum(padded, axis=1) // PIECE).astype(i32),
        zero_dest=zero_dest, n_zero=tail_end[-1:].astype(i32),
        tile_expert=jnp.minimum(tile_expert, N_EXPERTS - 1),
        n_tiles=(jnp.sum(region) // FFN_TM).astype(i32).reshape(1))


def _block_row(expert, rank, loc_ref, block):
    start = jnp.zeros_like(rank)
    for e in range(N_EXPERTS):
        start = jnp.where(expert == float(e), loc_ref[block * N_EXPERTS + e].astype(F32), start)
    return start + rank


def _row_tags(cols):
    lanes = []
    for w_col in (cols[:, 0:1], cols[:, 1:2]):
        hi = w_col.astype(BF16).astype(F32)
        mid = (w_col - hi).astype(BF16).astype(F32)
        lanes += [hi, mid, w_col - hi - mid]
    lanes += [cols[:, 2:3], cols[:, 3:4]]
    lane = lax.broadcasted_iota(jnp.int32, (cols.shape[0], 128), 1)
    tags = jnp.zeros((cols.shape[0], 128), F32)
    for i, v in enumerate(lanes):
        tags = jnp.where(lane == i, v, tags)
    return tags.astype(BF16)


def _for_each(n, fn):
    lax.fori_loop(0, n, lambda k, c: (fn(k), c)[1], 0)


def _dispatch_body(loc_ref, dest_ref, npieces_ref, zdest_ref, nzero_ref, ntiles_ref,
                   rows_ref, cols_ref, h_ref, xs_hbm, buf_ref, zero_ref, sem, zero_sem):
    b = pl.program_id(0)
    slot = b % 2

    def piece_copy(k, block, s):
        src = buf_ref.at[s, pl.ds(pl.multiple_of(k * PIECE, PIECE), PIECE), :]
        dst = xs_hbm.at[pl.ds(pl.multiple_of(dest_ref[block * BLOCK_PIECES + k], PIECE), PIECE), :]
        return pltpu.make_async_copy(src, dst, sem.at[s])

    def zero_copy(j):
        dst = xs_hbm.at[pl.ds(pl.multiple_of(zdest_ref[j], PIECE), PIECE), :]
        return pltpu.make_async_copy(zero_ref.at[pl.ds(0, PIECE), :], dst, zero_sem)

    def zero_tile_copy(t):
        dst = xs_hbm.at[pl.ds(pl.multiple_of(t * FFN_TM, FFN_TM), FFN_TM), :]
        return pltpu.make_async_copy(zero_ref, dst, zero_sem)

    idle_tiles = xs_hbm.shape[0] // FFN_TM - ntiles_ref[0]

    @pl.when(b == 0)
    def _start_zero_fill():
        zero_ref[...] = jnp.zeros_like(zero_ref)
        _for_each(nzero_ref[0], lambda j: zero_copy(j).start())
        _for_each(idle_tiles, lambda t: zero_tile_copy(ntiles_ref[0] + t).start())

    idx1 = _block_row(rows_ref[2:3, :], rows_ref[4:5, :], loc_ref, b)
    idx2 = _block_row(rows_ref[3:4, :], rows_ref[5:6, :], loc_ref, b)
    row = lax.broadcasted_iota(jnp.int32, (BLOCK_ROWS, MERGE_TM), 0).astype(F32)
    permute = jnp.where((row == idx1) | (row == idx2), 1.0, 0.0).astype(BF16)
    buf_ref[slot, :, 0:D_MODEL] = _dot(permute, h_ref[...]).astype(BF16)
    buf_ref[slot, :, D_MODEL:XS_W] = _dot(permute, _row_tags(cols_ref[...])).astype(BF16)
    _for_each(npieces_ref[b], lambda k: piece_copy(k, b, slot).start())

    def wait_block(block, s):
        pltpu.make_async_copy(buf_ref.at[s, pl.ds(0, 2 * MERGE_TM), :],
                              xs_hbm.at[pl.ds(0, 2 * MERGE_TM), :], sem.at[s]).wait()
        _for_each(npieces_ref[block] - 2 * MERGE_TM // PIECE,
                  lambda k: piece_copy(k, block, s).wait())

    @pl.when(b > 0)
    def _wait_previous_block():
        wait_block(b - 1, 1 - slot)

    @pl.when(b == pl.num_programs(0) - 1)
    def _wait_last_block():
        wait_block(b, slot)

    @pl.when(b == 0)
    def _wait_zero_fill():
        _for_each(nzero_ref[0], lambda j: zero_copy(j).wait())
        _for_each(idle_tiles, lambda t: zero_tile_copy(ntiles_ref[0] + t).wait())


def _dispatch(plan, route_rows, route, h2):
    n_blocks = route_rows.shape[0]
    rows_max = _max_ffn_tiles(n_blocks) * FFN_TM
    return pl.pallas_call(
        _dispatch_body,
        grid_spec=pltpu.PrefetchScalarGridSpec(
            num_scalar_prefetch=6,
            grid=(n_blocks,),
            in_specs=[pl.BlockSpec((None, 8, MERGE_TM), lambda i, *_: (i, 0, 0)),
                      pl.BlockSpec((MERGE_TM, ROUTE_W), lambda i, *_: (i, 0)),
                      pl.BlockSpec((MERGE_TM, D_MODEL), lambda i, *_: (i, 0))],
            out_specs=pl.BlockSpec(memory_space=pl.ANY),
            scratch_shapes=[pltpu.VMEM((2, BLOCK_ROWS, XS_W), BF16),
                            pltpu.VMEM((FFN_TM, XS_W), BF16),
                            pltpu.SemaphoreType.DMA((2,)),
                            pltpu.SemaphoreType.DMA(())]),
        out_shape=jax.ShapeDtypeStruct((rows_max, XS_W), BF16),
        compiler_params=pltpu.CompilerParams(
            dimension_semantics=("arbitrary",), vmem_limit_bytes=VMEM_LIMIT),
        name="moe_dispatch",
    )(plan["loc"], plan["dest"], plan["n_pieces"], plan["zero_dest"], plan["n_zero"],
      plan["n_tiles"], route_rows, route, h2)


def _ffn_body(tile_expert_ref, ntiles_ref, xs_ref, wg_ref, wu_ref, wd_ref, ys_ref,
              wg_bf_ref, wu_bf_ref, wd_bf_ref):
    i = pl.program_id(0)
    active = i < ntiles_ref[0]
    expert = tile_expert_ref[i]

    @pl.when(jnp.logical_not(active))
    def _idle_tile():
        ys_ref[...] = jnp.zeros_like(ys_ref)

    @pl.when(active & ((i == 0) | (expert != tile_expert_ref[jnp.maximum(i - 1, 0)])))
    def _cast_expert_weights():
        wg_bf_ref[...] = wg_ref[...].astype(BF16)
        wu_bf_ref[...] = wu_ref[...].astype(BF16)
        wd_bf_ref[...] = wd_ref[...].astype(BF16)

    @pl.when(active)
    def _run_tile():
        halves = [slice(0, FFN_TM // 2), slice(FFN_TM // 2, FFN_TM)]
        gate_up = [(_dot(xs_ref[rows, 0:D_MODEL], wg_bf_ref[...]),
                    _dot(xs_ref[rows, 0:D_MODEL], wu_bf_ref[...])) for rows in halves]
        for rows, (g, u) in zip(halves, gate_up):
            tags = xs_ref[rows, D_MODEL:XS_W].astype(F32)
            first_pick = tags[:, 6:7] == expert.astype(F32)
            weight = jnp.where(first_pick, tags[:, 0:1] + tags[:, 1:2] + tags[:, 2:3],
                               tags[:, 3:4] + tags[:, 4:5] + tags[:, 5:6])
            a = jax.nn.silu(g) * u * weight
            ys_ref[rows, :] = _dot(a.astype(BF16), wd_bf_ref[...]).astype(BF16)


def _expert_ffn(plan, xs, wg, wu, wd, layer):
    n_tiles_max = xs.shape[0] // FFN_TM

    def tile(i, tile_expert, n_tiles):
        return jnp.minimum(i, n_tiles[0] - 1)

    def expert(shape):
        return pl.BlockSpec((None, None) + shape,
                            lambda i, te, nt: (layer, te[tile(i, te, nt)], 0, 0))

    return pl.pallas_call(
        _ffn_body,
        grid_spec=pltpu.PrefetchScalarGridSpec(
            num_scalar_prefetch=2,
            grid=(n_tiles_max,),
            in_specs=[pl.BlockSpec((FFN_TM, XS_W), lambda i, te, nt: (tile(i, te, nt), 0)),
                      expert((D_MODEL, D_EXPERT)), expert((D_MODEL, D_EXPERT)),
                      expert((D_EXPERT, D_MODEL))],
            out_specs=pl.BlockSpec((FFN_TM, D_MODEL), lambda i, te, nt: (i, 0)),
            scratch_shapes=[pltpu.VMEM((D_MODEL, D_EXPERT), BF16),
                            pltpu.VMEM((D_MODEL, D_EXPERT), BF16),
                            pltpu.VMEM((D_EXPERT, D_MODEL), BF16)]),
        out_shape=jax.ShapeDtypeStruct((xs.shape[0], D_MODEL), BF16),
        compiler_params=pltpu.CompilerParams(
            dimension_semantics=("arbitrary",), vmem_limit_bytes=VMEM_LIMIT),
        name="moe_expert_ffn",
    )(plan["tile_expert"], plan["n_tiles"], xs, wg, wu, wd)


def _combine_body(loc_ref, dest_ref, npieces_ref, cols_ref, x1_ref, gfin_ref, ys_hbm, o_ref,
                  buf_ref, sem, *, final_norm):
    b = pl.program_id(0)
    slot = b % 2

    def piece_copy(k, block, s):
        src = ys_hbm.at[pl.ds(pl.multiple_of(dest_ref[block * BLOCK_PIECES + k], PIECE), PIECE), :]
        dst = buf_ref.at[s, pl.ds(pl.multiple_of(k * PIECE, PIECE), PIECE), :]
        return pltpu.make_async_copy(src, dst, sem.at[s])

    @pl.when(b == 0)
    def _first_block():
        buf_ref[...] = jnp.zeros_like(buf_ref)
        _for_each(npieces_ref[0], lambda k: piece_copy(k, 0, 0).start())

    @pl.when(b + 1 < pl.num_programs(0))
    def _prefetch_next_block():
        _for_each(npieces_ref[b + 1], lambda k: piece_copy(k, b + 1, 1 - slot).start())

    pltpu.make_async_copy(ys_hbm.at[pl.ds(0, 2 * MERGE_TM), :],
                          buf_ref.at[slot, pl.ds(0, 2 * MERGE_TM), :], sem.at[slot]).wait()
    _for_each(npieces_ref[b] - 2 * MERGE_TM // PIECE, lambda k: piece_copy(k, b, slot).wait())

    idx1 = _block_row(cols_ref[:, 2:3], cols_ref[:, 4:5], loc_ref, b)
    idx2 = _block_row(cols_ref[:, 3:4], cols_ref[:, 5:6], loc_ref, b)
    col = lax.broadcasted_iota(jnp.int32, (MERGE_TM, BLOCK_ROWS), 1).astype(F32)
    unpermute = jnp.where((col == idx1) | (col == idx2), 1.0, 0.0).astype(BF16)
    out = x1_ref[...] + _dot(unpermute, buf_ref[slot])
    o_ref[...] = _rms(out, gfin_ref[...]) if final_norm else out


def _combine(plan, route, x1, ys, g_final, final_norm):
    n = x1.shape[0]
    return pl.pallas_call(
        functools.partial(_combine_body, final_norm=final_norm),
        grid_spec=pltpu.PrefetchScalarGridSpec(
            num_scalar_prefetch=3,
            grid=(n // MERGE_TM,),
            in_specs=[pl.BlockSpec((MERGE_TM, ROUTE_W), lambda i, *_: (i, 0)),
                      pl.BlockSpec((MERGE_TM, D_MODEL), lambda i, *_: (i, 0)),
                      pl.BlockSpec((1, D_MODEL), lambda i, *_: (0, 0)),
                      pl.BlockSpec(memory_space=pl.ANY)],
            out_specs=pl.BlockSpec((MERGE_TM, D_MODEL), lambda i, *_: (i, 0)),
            scratch_shapes=[pltpu.VMEM((2, BLOCK_ROWS, D_MODEL), BF16),
                            pltpu.SemaphoreType.DMA((2,))]),
        out_shape=jax.ShapeDtypeStruct((n, D_MODEL), F32),
        compiler_params=pltpu.CompilerParams(
            dimension_semantics=("arbitrary",), vmem_limit_bytes=VMEM_LIMIT),
        name="moe_combine",
    )(plan["loc"], plan["dest"], plan["n_pieces"], route, x1, g_final.reshape(1, D_MODEL), ys)


def kernel(x, mem, g_mix, w_in, da_lambda, da_sub_g, hg_lower_bounds, hg_out_g, g_mem, w_mem_kv, w_branch, w_out, g_ffn, w_router, router_bias, w_exp_gate, w_exp_up, w_exp_down, g_final):
    b, s, d = x.shape
    m = mem.shape[1]
    n = b * s
    wr_pad = jnp.pad(w_router.astype(F32), ((0, 0), (0, ROUTE_W - N_EXPERTS)))
    wr_hi, wr_lo = _split_bf16(wr_pad)
    x2d = x.reshape(n, d)
    mem2d = mem.reshape(b * m, d)
    for l in range(DEPTH):
        lam_init = 0.8 - 0.6 * math.exp(-0.3 * l)
        proj = _norm_proj(x2d, g_mix[l], w_in[l].astype(BF16), row_tile=512, name="in_proj")
        proj3 = proj.reshape(b, s, IN_TOTAL)
        mkv = _norm_proj(mem2d, g_mem[l], w_mem_kv[l].astype(BF16), row_tile=512,
                         name="mem_kv_proj")
        y_a = _diff_attention(proj3, da_lambda[l].astype(F32), da_sub_g[l], lam_init)
        y_b = _hgrn2(proj3, hg_lower_bounds.astype(F32), hg_out_g[l], l)
        y_c = _cross_attention(proj3, mkv.reshape(b, m, 2 * MIX_W))
        x1, h2, route, route_rows, counts = _merge(
            y_a.reshape(n, MIX_W), y_b.reshape(n, MIX_W), y_c.reshape(n, MIX_W), proj, x2d,
            w_branch[l].astype(BF16), w_out[l].astype(BF16), g_ffn[l],
            wr_hi, wr_lo, router_bias.astype(F32))
        plan = _dispatch_plan(counts)
        xs = _dispatch(plan, route_rows, route, h2)
        ys = _expert_ffn(plan, xs, w_exp_gate, w_exp_up, w_exp_down, l)
        x2d = _combine(plan, route, x1, ys, g_final, final_norm=(l == DEPTH - 1))
    return x2d.reshape(b, s, d)
```

```python
import functools
import math

import numpy as np
import jax
import jax.numpy as jnp
from jax import lax
from jax.experimental import pallas as pl
from jax.experimental.pallas import tpu as pltpu

F32 = jnp.float32
BF16 = jnp.bfloat16

D_MODEL = 1024
DEPTH = 2
DA_HEADS = 4
DA_QK_DIM = 64
HEAD_W = 128
MIX_W = 512
HG_HEADS = 4
HG_CHUNK = 64
CA_HEADS = 4
N_BRANCH = 3
IN_TOTAL = 8 * MIX_W + N_BRANCH * D_MODEL
N_EXPERTS = 16
N_GROUPS = 4
EXPERTS_PER_GROUP = 4
D_EXPERT = 512
EPS = 1e-6

COL_DA_Q, COL_DA_K, COL_DA_V, COL_HG_Q, COL_HG_F, COL_HG_I, COL_HG_G, COL_CA_Q = range(8)
COL_GATES = 8 * MIX_W

VMEM_LIMIT = 56 * 1024 * 1024

NEG_INF = float("-inf")


def _dot(a, b):
    return jnp.dot(a, b, preferred_element_type=F32)


def _dot_nt(a, b):
    return lax.dot_general(a, b, (((1,), (1,)), ((), ())), preferred_element_type=F32)


def _dot_tn(a, b):
    return lax.dot_general(a, b, (((0,), (0,)), ((), ())), preferred_element_type=F32)


def _rms(x, g):
    return x * lax.rsqrt(jnp.mean(x * x, axis=-1, keepdims=True) + EPS) * g


def _split_bf16(x):
    hi = x.astype(BF16)
    lo = (x - hi.astype(F32)).astype(BF16)
    return hi, lo


def _norm_proj_body(x_ref, g_ref, w_ref, o_ref, *, col_chunk):
    h = _rms(x_ref[...], g_ref[...]).astype(BF16)
    for c in range(o_ref.shape[1] // col_chunk):
        cs = slice(c * col_chunk, (c + 1) * col_chunk)
        o_ref[:, cs] = _dot(h, w_ref[:, cs]).astype(BF16)


def _norm_proj(x2d, g, w_bf16, *, row_tile, name):
    n, d = x2d.shape
    width = w_bf16.shape[1]
    return pl.pallas_call(
        functools.partial(_norm_proj_body, col_chunk=512),
        grid=(n // row_tile,),
        in_specs=[
            pl.BlockSpec((row_tile, d), lambda i: (i, 0)),
            pl.BlockSpec((1, d), lambda i: (0, 0)),
            pl.BlockSpec((d, width), lambda i: (0, 0), pipeline_mode=pl.Buffered(1)),
        ],
        out_specs=pl.BlockSpec((row_tile, width), lambda i: (i, 0)),
        out_shape=jax.ShapeDtypeStruct((n, width), BF16),
        compiler_params=pltpu.CompilerParams(
            dimension_semantics=("parallel",), vmem_limit_bytes=VMEM_LIMIT),
        name=name,
    )(x2d, g.reshape(1, d), w_bf16)


DA_TQ = 512
DA_TK = 256
DA_VROWS = HEAD_W + 16
LOG2E = math.log2(math.e)
DA_QSCALE = DA_QK_DIM ** -0.5 * LOG2E
ALIBI_SLOPES = tuple(2.0 ** (-8.0 * (i + 1) / DA_HEADS) for i in range(DA_HEADS))
assert all(math.frexp(s)[0] == 0.5 for s in ALIBI_SLOPES) and DA_TK <= 256


def _bf16_terms(x, n):
    terms, rest = [], np.float32(x)
    for _ in range(n):
        t = rest.astype(jnp.bfloat16)
        terms.append(float(t))
        rest = np.float32(rest - np.float32(t))
    return tuple(terms)


LOG2E_TERMS = _bf16_terms(LOG2E, 3)


def _own_half(idx, m):
    return idx < DA_QK_DIM if m == 0 else idx >= DA_QK_DIM


def _bias_base(m):
    return DA_QK_DIM * (1 - m)


DA_HEADS_PER_STEP = 2


def _diff_attn_body(q_ref, k_ref, v_ref, lam_ref, subg_ref, o_ref, k0_ref, k1_ref, vt_ref,
                    s_ref, max_ref, acc_ref, *, lam_init):
    qi = pl.program_id(2)
    seq = k_ref.shape[0]
    heads = range(DA_HEADS_PER_STEP)
    head_cols = [slice(hh * HEAD_W, (hh + 1) * HEAD_W) for hh in heads]

    def slope_of(hh):
        h = pl.program_id(1) * DA_HEADS_PER_STEP + hh
        return jnp.where(h == 0, ALIBI_SLOPES[0],
                         jnp.where(h == 1, ALIBI_SLOPES[1],
                                   jnp.where(h == 2, ALIBI_SLOPES[2],
                                             ALIBI_SLOPES[3]))).astype(F32)

    slopes = [slope_of(hh) for hh in heads]

    @pl.when(qi == 0)
    def _prepare_keys_and_values():
        lane = lax.broadcasted_iota(jnp.int32, (seq, HEAD_W), 1)
        row = lax.broadcasted_iota(jnp.int32, (seq, HEAD_W), 0)
        for hh in heads:
            k = k_ref[:, head_cols[hh]].astype(F32)
            local_bias = (row & (DA_TK - 1)).astype(F32) * slopes[hh]
            for m, ka_ref in ((0, k0_ref), (1, k1_ref)):
                base = _bias_base(m)
                slot = (lane >= base) & (lane < base + len(LOG2E_TERMS))
                ka_ref[hh] = jnp.where(_own_half(lane, m), k,
                                       jnp.where(slot, local_bias, 0.0)).astype(BF16)
            for j in range(seq // DA_TK):
                vt_ref[hh, j, 0:HEAD_W, :] = (
                    v_ref[j * DA_TK:(j + 1) * DA_TK, head_cols[hh]].astype(F32).T.astype(BF16))
                vt_ref[hh, j, HEAD_W:DA_VROWS, :] = jnp.ones((DA_VROWS - HEAD_W, DA_TK), BF16)

    feat = lax.broadcasted_iota(jnp.int32, (HEAD_W, DA_TQ), 0)

    def q_aug(qt, m):
        side = jnp.zeros((HEAD_W, DA_TQ), F32)
        for i, c in enumerate(LOG2E_TERMS):
            side = jnp.where(feat == _bias_base(m) + i, c, side)
        return jnp.where(_own_half(feat, m), qt, side).astype(BF16)

    q_maps = []
    for hh in heads:
        qt = (q_ref[:, head_cols[hh]].astype(F32) * DA_QSCALE).T
        q_maps.append((q_aug(qt, 0), q_aug(qt, 1)))

    all_queries = slice(0, DA_TQ)

    def start_scores(j, slot, qs=all_queries):
        ks = pl.ds(pl.multiple_of(j * DA_TK, DA_TK), DA_TK)
        for hh in heads:
            s_ref[hh, slot, 0, :, qs] = _dot(k0_ref[hh, ks, :], q_maps[hh][0][:, qs])
            s_ref[hh, slot, 1, :, qs] = _dot(k1_ref[hh, ks, :], q_maps[hh][1][:, qs])

    def consume(j, slot, mask, qs=all_queries):
        for hh in heads:
            vt = vt_ref[hh, j]
            tile_bias = (slopes[hh] * LOG2E) * (j * DA_TK).astype(F32)
            for m in range(2):
                s = s_ref[hh, slot, m, :, qs]
                if mask is not None:
                    s = jnp.where(mask, s, NEG_INF)
                m_old = max_ref[hh, m, :, qs]
                m_new = jnp.maximum(m_old, jnp.max(s, axis=0, keepdims=True) + tile_bias)
                alpha = jnp.exp2(m_old - m_new)
                p = jnp.exp2(s - (m_new - tile_bias))
                max_ref[hh, m, :, qs] = m_new
                acc_ref[hh, m, :, qs] = (alpha * acc_ref[hh, m, :, qs]
                                         + _dot(vt, p.astype(BF16)))

    max_ref[...] = jnp.full(max_ref.shape, NEG_INF, F32)
    acc_ref[...] = jnp.zeros_like(acc_ref)

    tiles_per_q = DA_TQ // DA_TK
    assert tiles_per_q == 2
    n_full = qi * tiles_per_q
    start_scores(0, 0)

    def full_tile_pair(jj, _):
        j = jj * 2
        start_scores(j + 1, 1)
        consume(j, 0, None)
        start_scores(j + 2, 0)
        consume(j + 1, 1, None)
        return 0

    lax.fori_loop(0, qi, full_tile_pair, 0)
    late_queries = slice(DA_TK, DA_TQ)
    rk = lax.broadcasted_iota(jnp.int32, (DA_TK, DA_TQ), 0)
    cq = lax.broadcasted_iota(jnp.int32, (DA_TK, DA_TQ), 1)
    start_scores(n_full + 1, 1, late_queries)
    consume(n_full, 0, cq >= rk)
    consume(n_full + 1, 1, (cq >= rk)[:, 0:DA_TK], late_queries)

    lp = lam_ref[...]
    lam = (jnp.exp(jnp.sum(lp[0:1] * lp[1:2], axis=1, keepdims=True))
           - jnp.exp(jnp.sum(lp[2:3] * lp[3:4], axis=1, keepdims=True)) + lam_init)
    for hh in heads:
        a0, l0 = acc_ref[hh, 0, 0:HEAD_W, :], acc_ref[hh, 0, HEAD_W:HEAD_W + 1, :]
        a1, l1 = acc_ref[hh, 1, 0:HEAD_W, :], acc_ref[hh, 1, HEAD_W:HEAD_W + 1, :]
        o_t = a0 / l0 - lam * (a1 / l1)
        y_t = (o_t * lax.rsqrt(jnp.mean(o_t * o_t, axis=0, keepdims=True) + EPS)
               * subg_ref[...])
        o_ref[:, head_cols[hh]] = (y_t * (1.0 - lam_init)).T.astype(BF16)


def _diff_attention(proj3, lam_params, sub_g, lam_init):
    b, s, _ = proj3.shape
    hb = DA_HEADS_PER_STEP
    width = hb * HEAD_W
    per_stream = MIX_W // width
    return pl.pallas_call(
        functools.partial(_diff_attn_body, lam_init=lam_init),
        grid=(b, DA_HEADS // hb, s // DA_TQ),
        in_specs=[
            pl.BlockSpec((None, DA_TQ, width), lambda bi, g, i: (bi, i, COL_DA_Q * per_stream + g)),
            pl.BlockSpec((None, s, width), lambda bi, g, i: (bi, 0, COL_DA_K * per_stream + g)),
            pl.BlockSpec((None, s, width), lambda bi, g, i: (bi, 0, COL_DA_V * per_stream + g)),
            pl.BlockSpec((4, DA_QK_DIM), lambda bi, g, i: (0, 0)),
            pl.BlockSpec((HEAD_W, 1), lambda bi, g, i: (0, 0)),
        ],
        out_specs=pl.BlockSpec((None, DA_TQ, width), lambda bi, g, i: (bi, i, g)),
        out_shape=jax.ShapeDtypeStruct((b, s, MIX_W), BF16),
        scratch_shapes=[pltpu.VMEM((hb, s, HEAD_W), BF16), pltpu.VMEM((hb, s, HEAD_W), BF16),
                        pltpu.VMEM((hb, s // DA_TK, DA_VROWS, DA_TK), BF16),
                        pltpu.VMEM((hb, 2, 2, DA_TK, DA_TQ), F32),
                        pltpu.VMEM((hb, 2, 1, DA_TQ), F32),
                        pltpu.VMEM((hb, 2, DA_VROWS, DA_TQ), F32)],
        compiler_params=pltpu.CompilerParams(
            dimension_semantics=("parallel", "parallel", "arbitrary"),
            vmem_limit_bytes=VMEM_LIMIT),
        name="diff_attention",
    )(proj3, proj3, proj3, lam_params, sub_g.reshape(HEAD_W, 1))


HG_T = 512
HG_LEVELS = int(math.log2(HG_CHUNK))
HG_SUBLANES = 8
HG_FINE_LEVELS = tuple(lev for lev in range(HG_LEVELS)
                       if HG_CHUNK >> (lev + 1) < HG_SUBLANES)


def _hgrn_constants():
    c = HG_CHUNK
    r = np.arange(c)
    sums = np.zeros((1 + len(HG_FINE_LEVELS), c, c), np.float32)
    masks = np.zeros((HG_LEVELS + 1, c, c), np.float32)
    sums[0] = (r[None, :] <= r[:, None])
    for lev in range(HG_LEVELS):
        half = c >> (lev + 1)
        mid = (r // (2 * half)) * (2 * half) + half
        second = (r % (2 * half)) >= half
        rp = r[None, :]
        t_rows = second[:, None] & (rp >= mid[:, None]) & (rp <= r[:, None])
        s_rows = (~second)[:, None] & (rp > r[:, None]) & (rp < mid[:, None])
        if lev in HG_FINE_LEVELS:
            sums[1 + HG_FINE_LEVELS.index(lev)] = t_rows | s_rows
        same_block = (r[:, None] // (2 * half)) == (r[None, :] // (2 * half))
        masks[lev] = same_block & second[:, None] & (~second)[None, :]
    masks[HG_LEVELS] = np.eye(c)
    return sums.reshape(-1, c), masks


def _coarse_level_sums(bcum, half):
    parts = []
    for r0 in range(0, HG_CHUNK, HG_SUBLANES):
        mid = r0 // (2 * half) * (2 * half) + half
        ref = bcum[mid - 1:mid, :]
        rows = bcum[r0:r0 + HG_SUBLANES, :]
        parts.append(rows - ref if r0 % (2 * half) >= half else ref - rows)
    return jnp.concatenate(parts, axis=0)


def _layer_lower_bound(raw, layer):
    rows = [raw[i:i + 1] for i in range(raw.shape[0])]
    top = functools.reduce(jnp.maximum, rows)
    ex = [jnp.exp(r - top) for r in rows]
    tot = functools.reduce(jnp.add, ex)
    sm = [e / tot for e in ex]
    return functools.reduce(jnp.add, sm[:layer + 1]) - sm[0]


def _hgrn_body(q_ref, f_ref, i_ref, g_ref, lb_ref, outg_ref, sums_ref, masks_ref, o_ref,
               state_ref, *, layer):
    @pl.when(pl.program_id(1) == 0)
    def _reset_state():
        state_ref[...] = jnp.zeros_like(state_ref)

    c = HG_CHUNK
    lb = _layer_lower_bound(lb_ref[...], layer)
    sums = sums_ref[...]

    heads = [slice(h * HEAD_W, (h + 1) * HEAD_W) for h in range(HG_HEADS)]

    def gates_and_decays(ci):
        rows = pl.ds(pl.multiple_of(ci * c, c), c)
        f = lb + (1.0 - lb) * jax.nn.sigmoid(f_ref[rows, :].astype(F32))
        log_f = jnp.log(f)
        lf_hi, lf_lo = _split_bf16(log_f)
        fine = _dot(sums, lf_hi) + _dot(sums, lf_lo)
        bcum = fine[0:c, :]
        e_level = [jnp.exp(fine[(1 + HG_FINE_LEVELS.index(lev)) * c:
                                (2 + HG_FINE_LEVELS.index(lev)) * c, :])
                   if lev in HG_FINE_LEVELS else jnp.exp(_coarse_level_sums(bcum, c >> (lev + 1)))
                   for lev in range(HG_LEVELS)]
        return dict(rows=rows, kk=1.0 - f, qq=jax.nn.silu(q_ref[rows, :].astype(F32)),
                    e_cum=jnp.exp(bcum), e_tail=jnp.exp(bcum[c - 1:c, :] - bcum),
                    e_level=e_level, gate=jax.nn.silu(g_ref[rows, :].astype(F32)))

    def intra_chunk(ch):
        qq, kk = ch["qq"], ch["kk"]
        attn = []
        for cs in heads:
            a = masks_ref[HG_LEVELS] * _dot_nt(qq[:, cs].astype(BF16), kk[:, cs].astype(BF16))
            for lev in range(HG_LEVELS):
                e = ch["e_level"][lev][:, cs]
                a = a + masks_ref[lev] * _dot_nt((qq[:, cs] * e).astype(BF16),
                                                 (kk[:, cs] * e).astype(BF16))
            attn.append(a)
        return attn

    def carry_state(ch, attn):
        rows, e_cum = ch["rows"], ch["e_cum"]
        carried = [_dot_nt((ch["qq"][:, cs] * e_cum[:, cs]).astype(BF16),
                           state_ref[h].astype(BF16))
                   for h, cs in enumerate(heads)]
        update = [_dot_tn(i_ref[rows, cs], (ch["kk"][:, cs] * ch["e_tail"][:, cs]).astype(BF16))
                  for cs in heads]
        for h, cs in enumerate(heads):
            o = _dot(attn[h].astype(BF16), i_ref[rows, cs]) + carried[h]
            state_ref[h] = state_ref[h] * e_cum[c - 1:c, cs] + update[h]
            o_ref[rows, cs] = (_rms(o, outg_ref[...]) * ch["gate"][:, cs]).astype(BF16)

    def chunk_pair(cp, _):
        chunks = [gates_and_decays(2 * cp + u) for u in range(2)]
        attn = [intra_chunk(ch) for ch in chunks]
        for ch, a in zip(chunks, attn):
            carry_state(ch, a)
        return 0

    lax.fori_loop(0, HG_T // (2 * c), chunk_pair, 0)


def _hgrn2(proj3, lower_bounds, out_g, layer):
    b, s, _ = proj3.shape
    sums, masks = _hgrn_constants()

    def col(cblk):
        return pl.BlockSpec((None, HG_T, MIX_W), lambda bi, t: (bi, t, cblk))

    def whole(shape):
        return pl.BlockSpec(shape, lambda bi, t: (0,) * len(shape))

    return pl.pallas_call(
        functools.partial(_hgrn_body, layer=layer),
        grid=(b, s // HG_T),
        in_specs=[col(COL_HG_Q), col(COL_HG_F), col(COL_HG_I), col(COL_HG_G),
                  whole(lower_bounds.shape), whole((1, HEAD_W)),
                  whole(sums.shape), whole(masks.shape)],
        out_specs=pl.BlockSpec((None, HG_T, MIX_W), lambda bi, t: (bi, t, 0)),
        out_shape=jax.ShapeDtypeStruct((b, s, MIX_W), BF16),
        scratch_shapes=[pltpu.VMEM((HG_HEADS, HEAD_W, HEAD_W), F32)],
        compiler_params=pltpu.CompilerParams(
            dimension_semantics=("parallel", "arbitrary"), vmem_limit_bytes=VMEM_LIMIT),
        name="hgrn2",
    )(proj3, proj3, proj3, proj3, lower_bounds, out_g.reshape(1, HEAD_W),
      jnp.asarray(sums, BF16), jnp.asarray(masks, F32))


CA_TQ = 512


def _cross_attn_body(q_ref, mk_ref, mv_ref, o_ref):
    for h in range(CA_HEADS):
        cs = slice(h * HEAD_W, (h + 1) * HEAD_W)
        s = _dot_nt(q_ref[:, cs], mk_ref[:, cs]) * (HEAD_W ** -0.5)
        p = jnp.exp(s - jnp.max(s, axis=1, keepdims=True))
        l = jnp.sum(p, axis=1, keepdims=True)
        o_ref[:, cs] = (_dot(p.astype(BF16), mv_ref[:, cs]) / l).astype(BF16)


def _cross_attention(proj3, mkv3):
    b, s, _ = proj3.shape
    m = mkv3.shape[1]
    return pl.pallas_call(
        _cross_attn_body,
        grid=(b, s // CA_TQ),
        in_specs=[
            pl.BlockSpec((None, CA_TQ, MIX_W), lambda bi, i: (bi, i, COL_CA_Q)),
            pl.BlockSpec((None, m, MIX_W), lambda bi, i: (bi, 0, 0)),
            pl.BlockSpec((None, m, MIX_W), lambda bi, i: (bi, 0, 1)),
        ],
        out_specs=pl.BlockSpec((None, CA_TQ, MIX_W), lambda bi, i: (bi, i, 0)),
        out_shape=jax.ShapeDtypeStruct((b, s, MIX_W), BF16),
        compiler_params=pltpu.CompilerParams(
            dimension_semantics=("parallel", "parallel"), vmem_limit_bytes=VMEM_LIMIT),
        name="cross_attention",
    )(proj3, mkv3, mkv3)


MERGE_TM = 512
ROUTE_W = 128


def _top2_of_4(r0, r1, r2, r3):
    hi1, lo1 = jnp.maximum(r0, r1), jnp.minimum(r0, r1)
    hi2, lo2 = jnp.maximum(r2, r3), jnp.minimum(r2, r3)
    return jnp.maximum(hi1, hi2), jnp.maximum(jnp.minimum(hi1, hi2), jnp.maximum(lo1, lo2))


def _argmax_first(vals):
    best_v, best_i = vals[0], jnp.zeros_like(vals[0])
    for i in range(1, len(vals)):
        upd = vals[i] > best_v
        best_i = jnp.where(upd, float(i), best_i)
        best_v = jnp.where(upd, vals[i], best_v)
    return best_i


def _pick(idx, vals):
    out = vals[0]
    for i in range(1, len(vals)):
        out = jnp.where(idx == float(i), vals[i], out)
    return out


def _route(logits_t, bias_col):
    scores = jax.nn.sigmoid(logits_t)
    sel = scores + bias_col
    sel_rows = [sel[e:e + 1, :] for e in range(N_EXPERTS)]
    score_rows = [scores[e:e + 1, :] for e in range(N_EXPERTS)]
    grp = []
    for g in range(N_GROUPS):
        m1, m2 = _top2_of_4(*sel_rows[4 * g:4 * g + 4])
        grp.append(m1 + m2)
    best = _argmax_first(grp)
    cand = [_pick(best, [sel_rows[4 * g + i] for g in range(N_GROUPS)])
            for i in range(EXPERTS_PER_GROUP)]
    cand_score = [_pick(best, [score_rows[4 * g + i] for g in range(N_GROUPS)])
                  for i in range(EXPERTS_PER_GROUP)]
    i1 = _argmax_first(cand)
    i2 = _argmax_first([jnp.where(i1 == float(i), NEG_INF, cand[i])
                        for i in range(EXPERTS_PER_GROUP)])
    w1, w2 = _pick(i1, cand_score), _pick(i2, cand_score)
    tot = w1 + w2
    return w1 / tot, w2 / tot, best * 4.0 + i1, best * 4.0 + i2


def _merge_body(ya_ref, yb_ref, yc_ref, ga_ref, gb_ref, gc_ref, x_ref, wb_ref, wo_ref,
                gffn_ref, wr_hi_ref, wr_lo_ref, rbias_ref, tri_ref,
                x1_ref, h2_ref, route_ref, route_rows_ref, counts_ref):
    halves = [slice(0, MERGE_TM // 2), slice(MERGE_TM // 2, MERGE_TM)]
    mixers = ((ya_ref, ga_ref), (yb_ref, gb_ref), (yc_ref, gc_ref))
    branch = [[_dot(y_ref[rows, :], wb_ref[i]) for i, (y_ref, _) in enumerate(mixers)]
              for rows in halves]
    merged = [sum(jax.nn.sigmoid(gate_ref[rows, :].astype(F32)) * branch[hf][i]
                  for i, (_, gate_ref) in enumerate(mixers))
              for hf, rows in enumerate(halves)]
    x1 = [x_ref[rows, :] + _dot(merged[hf].astype(BF16), wo_ref[...])
          for hf, rows in enumerate(halves)]
    logits = []
    for hf, rows in enumerate(halves):
        x1_ref[rows, :] = x1[hf]
        h2 = _rms(x1[hf], gffn_ref[...])
        h2_ref[rows, :] = h2.astype(BF16)
        h_hi, h_lo = _split_bf16(h2)
        logits.append(_dot(h_hi, wr_hi_ref[...])
                      + (_dot(h_hi, wr_lo_ref[...]) + _dot(h_lo, wr_hi_ref[...])))
    logits_t = jnp.concatenate(logits, axis=0).T[0:N_EXPERTS, :]
    w1, w2, e1, e2 = _route(logits_t, rbias_ref[...])
    expert = lax.broadcasted_iota(jnp.int32, (N_EXPERTS, MERGE_TM), 0).astype(F32)
    pick1 = jnp.where(expert == e1, 1.0, 0.0)
    pick2 = jnp.where(expert == e2, 1.0, 0.0)
    picked = pick1 + pick2
    before = _dot(picked.astype(BF16), tri_ref[...])
    r1 = jnp.sum(pick1 * before, axis=0, keepdims=True)
    r2 = jnp.sum(pick2 * before, axis=0, keepdims=True)
    counts_ref[...] = jnp.broadcast_to(jnp.sum(picked, axis=1, keepdims=True),
                                       counts_ref.shape)
    fields = (w1, w2, e1, e2, r1, r2)
    row = lax.broadcasted_iota(jnp.int32, (ROUTE_W, MERGE_TM), 0)
    packed = jnp.zeros((ROUTE_W, MERGE_TM), F32)
    for i, f in enumerate(fields):
        packed = jnp.where(row == i, f, packed)
    route_ref[...] = packed.T
    route_rows_ref[...] = packed[0:8, :]


def _merge(ya, yb, yc, proj, x2d, w_branch, w_out, g_ffn, wr_hi, wr_lo, rbias):
    n = x2d.shape[0]
    tm = MERGE_TM

    def rows(width, cblk=0):
        return pl.BlockSpec((tm, width), lambda i: (i, cblk))

    def whole(shape):
        return pl.BlockSpec(shape, lambda i: (0,) * len(shape), pipeline_mode=pl.Buffered(1))

    return pl.pallas_call(
        _merge_body,
        grid=(n // tm,),
        in_specs=[rows(MIX_W), rows(MIX_W), rows(MIX_W),
                  rows(D_MODEL, COL_GATES // D_MODEL), rows(D_MODEL, COL_GATES // D_MODEL + 1),
                  rows(D_MODEL, COL_GATES // D_MODEL + 2),
                  rows(D_MODEL),
                  whole(w_branch.shape), whole(w_out.shape), whole((1, D_MODEL)),
                  whole(wr_hi.shape), whole(wr_lo.shape), whole((N_EXPERTS, 1)),
                  whole((tm, tm))],
        out_specs=[rows(D_MODEL), rows(D_MODEL), rows(ROUTE_W),
                   pl.BlockSpec((None, 8, tm), lambda i: (i, 0, 0)),
                   pl.BlockSpec((None, N_EXPERTS, ROUTE_W), lambda i: (i, 0, 0))],
        out_shape=[jax.ShapeDtypeStruct((n, D_MODEL), F32),
                   jax.ShapeDtypeStruct((n, D_MODEL), BF16),
                   jax.ShapeDtypeStruct((n, ROUTE_W), F32),
                   jax.ShapeDtypeStruct((n // tm, 8, tm), F32),
                   jax.ShapeDtypeStruct((n // tm, N_EXPERTS, ROUTE_W), F32)],
        compiler_params=pltpu.CompilerParams(
            dimension_semantics=("parallel",), vmem_limit_bytes=VMEM_LIMIT),
        name="merge_route",
    )(ya, yb, yc, proj, proj, proj, x2d, w_branch, w_out, g_ffn.reshape(1, D_MODEL), wr_hi, wr_lo,
      rbias.reshape(N_EXPERTS, 1),
      jnp.asarray(np.triu(np.ones((tm, tm), np.float32), 1), BF16))


PIECE = 16
BLOCK_ROWS = -(-(2 * MERGE_TM + N_EXPERTS * (PIECE - 1)) // 256) * 256
BLOCK_PIECES = BLOCK_ROWS // PIECE
FFN_TM = 512
XS_W = D_MODEL + 128
TAIL_PIECES = FFN_TM // PIECE


def _max_ffn_tiles(n_blocks):
    rows = n_blocks * (2 * MERGE_TM + N_EXPERTS * (PIECE - 1)) + N_EXPERTS * (FFN_TM - PIECE)
    return -(-rows // FFN_TM)


def _dispatch_plan(counts_out):
    i32 = jnp.int32
    counts = counts_out[:, :, 0].astype(i32)
    n_blocks = counts.shape[0]
    padded = (counts + PIECE - 1) // PIECE * PIECE
    loc = jnp.cumsum(padded, axis=1) - padded
    tot = jnp.sum(padded, axis=0)
    region = (tot + FFN_TM - 1) // FFN_TM * FFN_TM
    off = jnp.cumsum(region) - region
    seg = off[None, :] + jnp.cumsum(padded, axis=0) - padded
    experts = jnp.arange(N_EXPERTS, dtype=i32)

    def pick(table, e_idx):
        return jnp.sum(jnp.where(e_idx[..., None] == experts, table, 0), axis=-1, dtype=i32)

    k_row = jnp.arange(BLOCK_PIECES, dtype=i32) * PIECE
    e_of = jnp.sum(k_row[None, :, None] >= (loc + padded)[:, None, :], axis=2, dtype=i32)
    dest = pick((seg - loc)[:, None, :], jnp.minimum(e_of, N_EXPERTS - 1)) + k_row[None, :]
    n_tail = (region - tot) // PIECE
    tail_end = jnp.cumsum(n_tail)
    j = jnp.arange(N_EXPERTS * TAIL_PIECES, dtype=i32)
    e_tail = jnp.minimum(jnp.sum(j[:, None] >= tail_end[None, :], axis=1, dtype=i32),
                         N_EXPERTS - 1)
    zero_dest = pick((off + tot - (tail_end - n_tail) * PIECE)[None, :], e_tail) + j * PIECE
    tile_row = jnp.arange(_max_ffn_tiles(n_blocks), dtype=i32) * FFN_TM
    tile_expert = jnp.sum(tile_row[:, None] >= (off + region)[None, :], axis=1, dtype=i32)
    return dict(
        loc=loc.reshape(-1), dest=dest.reshape(-1),
        n_pieces=(jnp.sum(padded, axis=1) // PIECE).astype(i32),
        zero_dest=zero_dest, n_zero=tail_end[-1:].astype(i32),
        tile_expert=jnp.minimum(tile_expert, N_EXPERTS - 1),
        n_tiles=(jnp.sum(region) // FFN_TM).astype(i32).reshape(1))


def _block_row(expert, rank, loc_ref, block):
    start = jnp.zeros_like(rank)
    for e in range(N_EXPERTS):
        start = jnp.where(expert == float(e), loc_ref[block * N_EXPERTS + e].astype(F32), start)
    return start + rank


def _row_tags(cols):
    lanes = []
    for w_col in (cols[:, 0:1], cols[:, 1:2]):
        hi = w_col.astype(BF16).astype(F32)
        mid = (w_col - hi).astype(BF16).astype(F32)
        lanes += [hi, mid, w_col - hi - mid]
    lanes += [cols[:, 2:3], cols[:, 3:4]]
    lane = lax.broadcasted_iota(jnp.int32, (cols.shape[0], 128), 1)
    tags = jnp.zeros((cols.shape[0], 128), F32)
    for i, v in enumerate(lanes):
        tags = jnp.where(lane == i, v, tags)
    return tags.astype(BF16)


def _for_each(n, fn):
    lax.fori_loop(0, n, lambda k, c: (fn(k), c)[1], 0)


def _dispatch_body(loc_ref, dest_ref, npieces_ref, zdest_ref, nzero_ref, ntiles_ref,
                   rows_ref, cols_ref, h_ref, xs_hbm, buf_ref, zero_ref, sem, zero_sem):
    b = pl.program_id(0)
    slot = b % 2

    def piece_copy(k, block, s):
        src = buf_ref.at[s, pl.ds(pl.multiple_of(k * PIECE, PIECE), PIECE), :]
        dst = xs_hbm.at[pl.ds(pl.multiple_of(dest_ref[block * BLOCK_PIECES + k], PIECE), PIECE), :]
        return pltpu.make_async_copy(src, dst, sem.at[s])

    def zero_copy(j):
        dst = xs_hbm.at[pl.ds(pl.multiple_of(zdest_ref[j], PIECE), PIECE), :]
        return pltpu.make_async_copy(zero_ref.at[pl.ds(0, PIECE), :], dst, zero_sem)

    def zero_tile_copy(t):
        dst = xs_hbm.at[pl.ds(pl.multiple_of(t * FFN_TM, FFN_TM), FFN_TM), :]
        return pltpu.make_async_copy(zero_ref, dst, zero_sem)

    idle_tiles = xs_hbm.shape[0] // FFN_TM - ntiles_ref[0]

    @pl.when(b == 0)
    def _start_zero_fill():
        zero_ref[...] = jnp.zeros_like(zero_ref)
        _for_each(nzero_ref[0], lambda j: zero_copy(j).start())
        _for_each(idle_tiles, lambda t: zero_tile_copy(ntiles_ref[0] + t).start())

    idx1 = _block_row(rows_ref[2:3, :], rows_ref[4:5, :], loc_ref, b)
    idx2 = _block_row(rows_ref[3:4, :], rows_ref[5:6, :], loc_ref, b)
    row = lax.broadcasted_iota(jnp.int32, (BLOCK_ROWS, MERGE_TM), 0).astype(F32)
    permute = jnp.where((row == idx1) | (row == idx2), 1.0, 0.0).astype(BF16)
    buf_ref[slot, :, 0:D_MODEL] = _dot(permute, h_ref[...]).astype(BF16)
    buf_ref[slot, :, D_MODEL:XS_W] = _dot(permute, _row_tags(cols_ref[...])).astype(BF16)
    _for_each(npieces_ref[b], lambda k: piece_copy(k, b, slot).start())

    def wait_block(block, s):
        pltpu.make_async_copy(buf_ref.at[s, pl.ds(0, 2 * MERGE_TM), :],
                              xs_hbm.at[pl.ds(0, 2 * MERGE_TM), :], sem.at[s]).wait()
        _for_each(npieces_ref[block] - 2 * MERGE_TM // PIECE,
                  lambda k: piece_copy(k, block, s).wait())

    @pl.when(b > 0)
    def _wait_previous_block():
        wait_block(b - 1, 1 - slot)

    @pl.when(b == pl.num_programs(0) - 1)
    def _wait_last_block():
        wait_block(b, slot)

    @pl.when(b == 0)
    def _wait_zero_fill():
        _for_each(nzero_ref[0], lambda j: zero_copy(j).wait())
        _for_each(idle_tiles, lambda t: zero_tile_copy(ntiles_ref[0] + t).wait())


def _dispatch(plan, route_rows, route, h2):
    n_blocks = route_rows.shape[0]
    rows_max = _max_ffn_tiles(n_blocks) * FFN_TM
    return pl.pallas_call(
        _dispatch_body,
        grid_spec=pltpu.PrefetchScalarGridSpec(
            num_scalar_prefetch=6,
            grid=(n_blocks,),
            in_specs=[pl.BlockSpec((None, 8, MERGE_TM), lambda i, *_: (i, 0, 0)),
                      pl.BlockSpec((MERGE_TM, ROUTE_W), lambda i, *_: (i, 0)),
                      pl.BlockSpec((MERGE_TM, D_MODEL), lambda i, *_: (i, 0))],
            out_specs=pl.BlockSpec(memory_space=pl.ANY),
            scratch_shapes=[pltpu.VMEM((2, BLOCK_ROWS, XS_W), BF16),
                            pltpu.VMEM((FFN_TM, XS_W), BF16),
                            pltpu.SemaphoreType.DMA((2,)),
                            pltpu.SemaphoreType.DMA(())]),
        out_shape=jax.ShapeDtypeStruct((rows_max, XS_W), BF16),
        compiler_params=pltpu.CompilerParams(
            dimension_semantics=("arbitrary",), vmem_limit_bytes=VMEM_LIMIT),
        name="moe_dispatch",
    )(plan["loc"], plan["dest"], plan["n_pieces"], plan["zero_dest"], plan["n_zero"],
      plan["n_tiles"], route_rows, route, h2)


def _ffn_body(tile_expert_ref, ntiles_ref, xs_ref, wg_ref, wu_ref, wd_ref, ys_ref,
              wg_bf_ref, wu_bf_ref, wd_bf_ref):
    i = pl.program_id(0)
    active = i < ntiles_ref[0]
    expert = tile_expert_ref[i]

    @pl.when(jnp.logical_not(active))
    def _idle_tile():
        ys_ref[...] = jnp.zeros_like(ys_ref)

    @pl.when(active & ((i == 0) | (expert != tile_expert_ref[jnp.maximum(i - 1, 0)])))
    def _cast_expert_weights():
        wg_bf_ref[...] = wg_ref[...].astype(BF16)
        wu_bf_ref[...] = wu_ref[...].astype(BF16)
        wd_bf_ref[...] = wd_ref[...].astype(BF16)

    @pl.when(active)
    def _run_tile():
        halves = [slice(0, FFN_TM // 2), slice(FFN_TM // 2, FFN_TM)]
        gate_up = [(_dot(xs_ref[rows, 0:D_MODEL], wg_bf_ref[...]),
                    _dot(xs_ref[rows, 0:D_MODEL], wu_bf_ref[...])) for rows in halves]
        for rows, (g, u) in zip(halves, gate_up):
            tags = xs_ref[rows, D_MODEL:XS_W].astype(F32)
            first_pick = tags[:, 6:7] == expert.astype(F32)
            weight = jnp.where(first_pick, tags[:, 0:1] + tags[:, 1:2] + tags[:, 2:3],
                               tags[:, 3:4] + tags[:, 4:5] + tags[:, 5:6])
            a = jax.nn.silu(g) * u * weight
            ys_ref[rows, :] = _dot(a.astype(BF16), wd_bf_ref[...]).astype(BF16)


def _expert_ffn(plan, xs, wg, wu, wd, layer):
    n_tiles_max = xs.shape[0] // FFN_TM

    def tile(i, tile_expert, n_tiles):
        return jnp.minimum(i, n_tiles[0] - 1)

    def expert(shape):
        return pl.BlockSpec((None, None) + shape,
                            lambda i, te, nt: (layer, te[tile(i, te, nt)], 0, 0))

    return pl.pallas_call(
        _ffn_body,
        grid_spec=pltpu.PrefetchScalarGridSpec(
            num_scalar_prefetch=2,
            grid=(n_tiles_max,),
            in_specs=[pl.BlockSpec((FFN_TM, XS_W), lambda i, te, nt: (tile(i, te, nt), 0)),
                      expert((D_MODEL, D_EXPERT)), expert((D_MODEL, D_EXPERT)),
                      expert((D_EXPERT, D_MODEL))],
            out_specs=pl.BlockSpec((FFN_TM, D_MODEL), lambda i, te, nt: (i, 0)),
            scratch_shapes=[pltpu.VMEM((D_MODEL, D_EXPERT), BF16),
                            pltpu.VMEM((D_MODEL, D_EXPERT), BF16),
                            pltpu.VMEM((D_EXPERT, D_MODEL), BF16)]),
        out_shape=jax.ShapeDtypeStruct((xs.shape[0], D_MODEL), BF16),
        compiler_params=pltpu.CompilerParams(
            dimension_semantics=("arbitrary",), vmem_limit_bytes=VMEM_LIMIT),
        name="moe_expert_ffn",
    )(plan["tile_expert"], plan["n_tiles"], xs, wg, wu, wd)


def _combine_body(loc_ref, dest_ref, npieces_ref, cols_ref, x1_ref, gfin_ref, ys_hbm, o_ref,
                  buf_ref, sem, *, final_norm):
    b = pl.program_id(0)
    slot = b % 2

    def piece_copy(k, block, s):
        src = ys_hbm.at[pl.ds(pl.multiple_of(dest_ref[block * BLOCK_PIECES + k], PIECE), PIECE), :]
        dst = buf_ref.at[s, pl.ds(pl.multiple_of(k * PIECE, PIECE), PIECE), :]
        return pltpu.make_async_copy(src, dst, sem.at[s])

    @pl.when(b == 0)
    def _first_block():
        buf_ref[...] = jnp.zeros_like(buf_ref)
        _for_each(npieces_ref[0], lambda k: piece_copy(k, 0, 0).start())

    @pl.when(b + 1 < pl.num_programs(0))
    def _prefetch_next_block():
        _for_each(npieces_ref[b + 1], lambda k: piece_copy(k, b + 1, 1 - slot).start())

    pltpu.make_async_copy(ys_hbm.at[pl.ds(0, 2 * MERGE_TM), :],
                          buf_ref.at[slot, pl.ds(0, 2 * MERGE_TM), :], sem.at[slot]).wait()
    _for_each(npieces_ref[b] - 2 * MERGE_TM // PIECE, lambda k: piece_copy(k, b, slot).wait())

    idx1 = _block_row(cols_ref[:, 2:3], cols_ref[:, 4:5], loc_ref, b)
    idx2 = _block_row(cols_ref[:, 3:4], cols_ref[:, 5:6], loc_ref, b)
    col = lax.broadcasted_iota(jnp.int32, (MERGE_TM, BLOCK_ROWS), 1).astype(F32)
    unpermute = jnp.where((col == idx1) | (col == idx2), 1.0, 0.0).astype(BF16)
    out = x1_ref[...] + _dot(unpermute, buf_ref[slot])
    o_ref[...] = _rms(out, gfin_ref[...]) if final_norm else out


def _combine(plan, route, x1, ys, g_final, final_norm):
    n = x1.shape[0]
    return pl.pallas_call(
        functools.partial(_combine_body, final_norm=final_norm),
        grid_spec=pltpu.PrefetchScalarGridSpec(
            num_scalar_prefetch=3,
            grid=(n // MERGE_TM,),
            in_specs=[pl.BlockSpec((MERGE_TM, ROUTE_W), lambda i, *_: (i, 0)),
                      pl.BlockSpec((MERGE_TM, D_MODEL), lambda i, *_: (i, 0)),
                      pl.BlockSpec((1, D_MODEL), lambda i, *_: (0, 0)),
                      pl.BlockSpec(memory_space=pl.ANY)],
            out_specs=pl.BlockSpec((MERGE_TM, D_MODEL), lambda i, *_: (i, 0)),
            scratch_shapes=[pltpu.VMEM((2, BLOCK_ROWS, D_MODEL), BF16),
                            pltpu.SemaphoreType.DMA((2,))]),
        out_shape=jax.ShapeDtypeStruct((n, D_MODEL), F32),
        compiler_params=pltpu.CompilerParams(
            dimension_semantics=("arbitrary",), vmem_limit_bytes=VMEM_LIMIT),
        name="moe_combine",
    )(plan["loc"], plan["dest"], plan["n_pieces"], route, x1, g_final.reshape(1, D_MODEL), ys)


def kernel(x, mem, g_mix, w_in, da_lambda, da_sub_g, hg_lower_bounds, hg_out_g, g_mem, w_mem_kv, w_branch, w_out, g_ffn, w_router, router_bias, w_exp_gate, w_exp_up, w_exp_down, g_final):
    b, s, d = x.shape
    m = mem.shape[1]
    n = b * s
    wr_pad = jnp.pad(w_router.astype(F32), ((0, 0), (0, ROUTE_W - N_EXPERTS)))
    wr_hi, wr_lo = _split_bf16(wr_pad)
    x2d = x.reshape(n, d)
    mem2d = mem.reshape(b * m, d)
    for l in range(DEPTH):
        lam_init = 0.8 - 0.6 * math.exp(-0.3 * l)
        proj = _norm_proj(x2d, g_mix[l], w_in[l].astype(BF16), row_tile=512, name="in_proj")
        proj3 = proj.reshape(b, s, IN_TOTAL)
        mkv = _norm_proj(mem2d, g_mem[l], w_mem_kv[l].astype(BF16), row_tile=512,
                         name="mem_kv_proj")
        y_a = _diff_attention(proj3, da_lambda[l].astype(F32), da_sub_g[l], lam_init)
        y_b = _hgrn2(proj3, hg_lower_bounds.astype(F32), hg_out_g[l], l)
        y_c = _cross_attention(proj3, mkv.reshape(b, m, 2 * MIX_W))
        x1, h2, route, route_rows, counts = _merge(
            y_a.reshape(n, MIX_W), y_b.reshape(n, MIX_W), y_c.reshape(n, MIX_W), proj, x2d,
            w_branch[l].astype(BF16), w_out[l].astype(BF16), g_ffn[l],
            wr_hi, wr_lo, router_bias.astype(F32))
        plan = _dispatch_plan(counts)
        xs = _dispatch(plan, route_rows, route, h2)
        ys = _expert_ffn(plan, xs, w_exp_gate, w_exp_up, w_exp_down, l)
        x2d = _combine(plan, route, x1, ys, g_final, final_norm=(l == DEPTH - 1))
    return x2d.reshape(b, s, d)
```

```python
import functools
import math

import numpy as np
import jax
import jax.numpy as jnp
from jax import lax
from jax.experimental import pallas as pl
from jax.experimental.pallas import tpu as pltpu

F32 = jnp.float32
BF16 = jnp.bfloat16

D_MODEL = 1024
DEPTH = 2
DA_HEADS = 4
DA_QK_DIM = 64
HEAD_W = 128
MIX_W = 512
HG_HEADS = 4
HG_CHUNK = 64
CA_HEADS = 4
N_BRANCH = 3
IN_TOTAL = 8 * MIX_W + N_BRANCH * D_MODEL
N_EXPERTS = 16
N_GROUPS = 4
EXPERTS_PER_GROUP = 4
D_EXPERT = 512
EPS = 1e-6

COL_DA_Q, COL_DA_K, COL_DA_V, COL_HG_Q, COL_HG_F, COL_HG_I, COL_HG_G, COL_CA_Q = range(8)
COL_GATES = 8 * MIX_W

VMEM_LIMIT = 56 * 1024 * 1024

NEG_INF = float("-inf")


def _dot(a, b):
    return jnp.dot(a, b, preferred_element_type=F32)


def _dot_nt(a, b):
    return lax.dot_general(a, b, (((1,), (1,)), ((), ())), preferred_element_type=F32)


def _dot_tn(a, b):
    return lax.dot_general(a, b, (((0,), (0,)), ((), ())), preferred_element_type=F32)


def _rms(x, g):
    return x * lax.rsqrt(jnp.mean(x * x, axis=-1, keepdims=True) + EPS) * g


def _split_bf16(x):
    hi = x.astype(BF16)
    lo = (x - hi.astype(F32)).astype(BF16)
    return hi, lo


def _norm_proj_body(x_ref, g_ref, w_ref, o_ref, *, col_chunk):
    h = _rms(x_ref[...], g_ref[...]).astype(BF16)
    for c in range(o_ref.shape[1] // col_chunk):
        cs = slice(c * col_chunk, (c + 1) * col_chunk)
        o_ref[:, cs] = _dot(h, w_ref[:, cs]).astype(BF16)


def _norm_proj(x2d, g, w_bf16, *, row_tile, name):
    n, d = x2d.shape
    width = w_bf16.shape[1]
    return pl.pallas_call(
        functools.partial(_norm_proj_body, col_chunk=512),
        grid=(n // row_tile,),
        in_specs=[
            pl.BlockSpec((row_tile, d), lambda i: (i, 0)),
            pl.BlockSpec((1, d), lambda i: (0, 0)),
            pl.BlockSpec((d, width), lambda i: (0, 0), pipeline_mode=pl.Buffered(1)),
        ],
        out_specs=pl.BlockSpec((row_tile, width), lambda i: (i, 0)),
        out_shape=jax.ShapeDtypeStruct((n, width), BF16),
        compiler_params=pltpu.CompilerParams(
            dimension_semantics=("parallel",), vmem_limit_bytes=VMEM_LIMIT),
        name=name,
    )(x2d, g.reshape(1, d), w_bf16)


DA_TQ = 512
DA_TK = 256
DA_VROWS = HEAD_W + 16
LOG2E = math.log2(math.e)
DA_QSCALE = DA_QK_DIM ** -0.5 * LOG2E
ALIBI_SLOPES = tuple(2.0 ** (-8.0 * (i + 1) / DA_HEADS) for i in range(DA_HEADS))
assert all(math.frexp(s)[0] == 0.5 for s in ALIBI_SLOPES) and DA_TK <= 256


def _bf16_terms(x, n):
    terms, rest = [], np.float32(x)
    for _ in range(n):
        t = rest.astype(jnp.bfloat16)
        terms.append(float(t))
        rest = np.float32(rest - np.float32(t))
    return tuple(terms)


LOG2E_TERMS = _bf16_terms(LOG2E, 3)


def _own_half(idx, m):
    return idx < DA_QK_DIM if m == 0 else idx >= DA_QK_DIM


def _bias_base(m):
    return DA_QK_DIM * (1 - m)


DA_HEADS_PER_STEP = 4


def _diff_attn_body(q_ref, k_ref, v_ref, lam_ref, subg_ref, o_ref, k0_ref, k1_ref, vt_ref,
                    s_ref, max_ref, acc_ref, *, lam_init):
    qi = pl.program_id(2)
    seq = k_ref.shape[0]
    heads = range(DA_HEADS_PER_STEP)
    head_cols = [slice(hh * HEAD_W, (hh + 1) * HEAD_W) for hh in heads]

    def slope_of(hh):
        h = pl.program_id(1) * DA_HEADS_PER_STEP + hh
        return jnp.where(h == 0, ALIBI_SLOPES[0],
                         jnp.where(h == 1, ALIBI_SLOPES[1],
                                   jnp.where(h == 2, ALIBI_SLOPES[2],
                                             ALIBI_SLOPES[3]))).astype(F32)

    slopes = [slope_of(hh) for hh in heads]

    @pl.when(qi == 0)
    def _prepare_keys_and_values():
        lane = lax.broadcasted_iota(jnp.int32, (seq, HEAD_W), 1)
        row = lax.broadcasted_iota(jnp.int32, (seq, HEAD_W), 0)
        for hh in heads:
            k = k_ref[:, head_cols[hh]].astype(F32)
            local_bias = (row & (DA_TK - 1)).astype(F32) * slopes[hh]
            for m, ka_ref in ((0, k0_ref), (1, k1_ref)):
                base = _bias_base(m)
                slot = (lane >= base) & (lane < base + len(LOG2E_TERMS))
                ka_ref[hh] = jnp.where(_own_half(lane, m), k,
                                       jnp.where(slot, local_bias, 0.0)).astype(BF16)
            for j in range(seq // DA_TK):
                vt_ref[hh, j, 0:HEAD_W, :] = (
                    v_ref[j * DA_TK:(j + 1) * DA_TK, head_cols[hh]].astype(F32).T.astype(BF16))
                vt_ref[hh, j, HEAD_W:DA_VROWS, :] = jnp.ones((DA_VROWS - HEAD_W, DA_TK), BF16)

    feat = lax.broadcasted_iota(jnp.int32, (HEAD_W, DA_TQ), 0)

    def q_aug(qt, m):
        side = jnp.zeros((HEAD_W, DA_TQ), F32)
        for i, c in enumerate(LOG2E_TERMS):
            side = jnp.where(feat == _bias_base(m) + i, c, side)
        return jnp.where(_own_half(feat, m), qt, side).astype(BF16)

    q_maps = []
    for hh in heads:
        qt = (q_ref[:, head_cols[hh]].astype(F32) * DA_QSCALE).T
        q_maps.append((q_aug(qt, 0), q_aug(qt, 1)))

    all_queries = slice(0, DA_TQ)

    def start_scores(j, slot, qs=all_queries):
        ks = pl.ds(pl.multiple_of(j * DA_TK, DA_TK), DA_TK)
        for hh in heads:
            s_ref[hh, slot, 0, :, qs] = _dot(k0_ref[hh, ks, :], q_maps[hh][0][:, qs])
            s_ref[hh, slot, 1, :, qs] = _dot(k1_ref[hh, ks, :], q_maps[hh][1][:, qs])

    def consume(j, slot, mask, qs=all_queries):
        for hh in heads:
            vt = vt_ref[hh, j]
            tile_bias = (slopes[hh] * LOG2E) * (j * DA_TK).astype(F32)
            for m in range(2):
                s = s_ref[hh, slot, m, :, qs]
                if mask is not None:
                    s = jnp.where(mask, s, NEG_INF)
                m_old = max_ref[hh, m, :, qs]
                m_new = jnp.maximum(m_old, jnp.max(s, axis=0, keepdims=True) + tile_bias)
                alpha = jnp.exp2(m_old - m_new)
                p = jnp.exp2(s - (m_new - tile_bias))
                max_ref[hh, m, :, qs] = m_new
                acc_ref[hh, m, :, qs] = (alpha * acc_ref[hh, m, :, qs]
                                         + _dot(vt, p.astype(BF16)))

    max_ref[...] = jnp.full(max_ref.shape, NEG_INF, F32)
    acc_ref[...] = jnp.zeros_like(acc_ref)

    tiles_per_q = DA_TQ // DA_TK
    assert tiles_per_q == 2
    n_full = qi * tiles_per_q
    start_scores(0, 0)

    def full_tile_pair(jj, _):
        j = jj * 2
        start_scores(j + 1, 1)
        consume(j, 0, None)
        start_scores(j + 2, 0)
        consume(j + 1, 1, None)
        return 0

    lax.fori_loop(0, qi, full_tile_pair, 0)
    late_queries = slice(DA_TK, DA_TQ)
    rk = lax.broadcasted_iota(jnp.int32, (DA_TK, DA_TQ), 0)
    cq = lax.broadcasted_iota(jnp.int32, (DA_TK, DA_TQ), 1)
    start_scores(n_full + 1, 1, late_queries)
    consume(n_full, 0, cq >= rk)
    consume(n_full + 1, 1, (cq >= rk)[:, 0:DA_TK], late_queries)

    lp = lam_ref[...]
    lam = (jnp.exp(jnp.sum(lp[0:1] * lp[1:2], axis=1, keepdims=True))
           - jnp.exp(jnp.sum(lp[2:3] * lp[3:4], axis=1, keepdims=True)) + lam_init)
    for hh in heads:
        a0, l0 = acc_ref[hh, 0, 0:HEAD_W, :], acc_ref[hh, 0, HEAD_W:HEAD_W + 1, :]
        a1, l1 = acc_ref[hh, 1, 0:HEAD_W, :], acc_ref[hh, 1, HEAD_W:HEAD_W + 1, :]
        o_t = a0 / l0 - lam * (a1 / l1)
        y_t = (o_t * lax.rsqrt(jnp.mean(o_t * o_t, axis=0, keepdims=True) + EPS)
               * subg_ref[...])
        o_ref[:, head_cols[hh]] = (y_t * (1.0 - lam_init)).T.astype(BF16)


def _diff_attention(proj3, lam_params, sub_g, lam_init):
    b, s, _ = proj3.shape
    hb = DA_HEADS_PER_STEP
    width = hb * HEAD_W
    per_stream = MIX_W // width
    return pl.pallas_call(
        functools.partial(_diff_attn_body, lam_init=lam_init),
        grid=(b, DA_HEADS // hb, s // DA_TQ),
        in_specs=[
            pl.BlockSpec((None, DA_TQ, width), lambda bi, g, i: (bi, i, COL_DA_Q * per_stream + g)),
            pl.BlockSpec((None, s, width), lambda bi, g, i: (bi, 0, COL_DA_K * per_stream + g)),
            pl.BlockSpec((None, s, width), lambda bi, g, i: (bi, 0, COL_DA_V * per_stream + g)),
            pl.BlockSpec((4, DA_QK_DIM), lambda bi, g, i: (0, 0)),
            pl.BlockSpec((HEAD_W, 1), lambda bi, g, i: (0, 0)),
        ],
        out_specs=pl.BlockSpec((None, DA_TQ, width), lambda bi, g, i: (bi, i, g)),
        out_shape=jax.ShapeDtypeStruct((b, s, MIX_W), BF16),
        scratch_shapes=[pltpu.VMEM((hb, s, HEAD_W), BF16), pltpu.VMEM((hb, s, HEAD_W), BF16),
                        pltpu.VMEM((hb, s // DA_TK, DA_VROWS, DA_TK), BF16),
                        pltpu.VMEM((hb, 2, 2, DA_TK, DA_TQ), F32),
                        pltpu.VMEM((hb, 2, 1, DA_TQ), F32),
                        pltpu.VMEM((hb, 2, DA_VROWS, DA_TQ), F32)],
        compiler_params=pltpu.CompilerParams(
            dimension_semantics=("parallel", "parallel", "arbitrary"),
            vmem_limit_bytes=VMEM_LIMIT),
        name="diff_attention",
    )(proj3, proj3, proj3, lam_params, sub_g.reshape(HEAD_W, 1))


HG_T = 1024
HG_LEVELS = int(math.log2(HG_CHUNK))
HG_SUBLANES = 8
HG_FINE_LEVELS = tuple(lev for lev in range(HG_LEVELS)
                       if HG_CHUNK >> (lev + 1) < HG_SUBLANES)


def _hgrn_constants():
    c = HG_CHUNK
    r = np.arange(c)
    sums = np.zeros((1 + len(HG_FINE_LEVELS), c, c), np.float32)
    masks = np.zeros((HG_LEVELS + 1, c, c), np.float32)
    sums[0] = (r[None, :] <= r[:, None])
    for lev in range(HG_LEVELS):
        half = c >> (lev + 1)
        mid = (r // (2 * half)) * (2 * half) + half
        second = (r % (2 * half)) >= half
        rp = r[None, :]
        t_rows = second[:, None] & (rp >= mid[:, None]) & (rp <= r[:, None])
        s_rows = (~second)[:, None] & (rp > r[:, None]) & (rp < mid[:, None])
        if lev in HG_FINE_LEVELS:
            sums[1 + HG_FINE_LEVELS.index(lev)] = t_rows | s_rows
        same_block = (r[:, None] // (2 * half)) == (r[None, :] // (2 * half))
        masks[lev] = same_block & second[:, None] & (~second)[None, :]
    masks[HG_LEVELS] = np.eye(c)
    return sums.reshape(-1, c), masks


def _coarse_level_sums(bcum, half):
    parts = []
    for r0 in range(0, HG_CHUNK, HG_SUBLANES):
        mid = r0 // (2 * half) * (2 * half) + half
        ref = bcum[mid - 1:mid, :]
        rows = bcum[r0:r0 + HG_SUBLANES, :]
        parts.append(rows - ref if r0 % (2 * half) >= half else ref - rows)
    return jnp.concatenate(parts, axis=0)


def _layer_lower_bound(raw, layer):
    rows = [raw[i:i + 1] for i in range(raw.shape[0])]
    top = functools.reduce(jnp.maximum, rows)
    ex = [jnp.exp(r - top) for r in rows]
    tot = functools.reduce(jnp.add, ex)
    sm = [e / tot for e in ex]
    return functools.reduce(jnp.add, sm[:layer + 1]) - sm[0]


def _hgrn_body(q_ref, f_ref, i_ref, g_ref, lb_ref, outg_ref, sums_ref, masks_ref, o_ref,
               state_ref, *, layer):
    @pl.when(pl.program_id(1) == 0)
    def _reset_state():
        state_ref[...] = jnp.zeros_like(state_ref)

    c = HG_CHUNK
    lb = _layer_lower_bound(lb_ref[...], layer)
    sums = sums_ref[...]

    heads = [slice(h * HEAD_W, (h + 1) * HEAD_W) for h in range(HG_HEADS)]

    def gates_and_decays(ci):
        rows = pl.ds(pl.multiple_of(ci * c, c), c)
        f = lb + (1.0 - lb) * jax.nn.sigmoid(f_ref[rows, :].astype(F32))
        log_f = jnp.log(f)
        lf_hi, lf_lo = _split_bf16(log_f)
        fine = _dot(sums, lf_hi) + _dot(sums, lf_lo)
        bcum = fine[0:c, :]
        e_level = [jnp.exp(fine[(1 + HG_FINE_LEVELS.index(lev)) * c:
                                (2 + HG_FINE_LEVELS.index(lev)) * c, :])
                   if lev in HG_FINE_LEVELS else jnp.exp(_coarse_level_sums(bcum, c >> (lev + 1)))
                   for lev in range(HG_LEVELS)]
        return dict(rows=rows, kk=1.0 - f, qq=jax.nn.silu(q_ref[rows, :].astype(F32)),
                    e_cum=jnp.exp(bcum), e_tail=jnp.exp(bcum[c - 1:c, :] - bcum),
                    e_level=e_level, gate=jax.nn.silu(g_ref[rows, :].astype(F32)))

    def intra_chunk(ch):
        qq, kk = ch["qq"], ch["kk"]
        attn = []
        for cs in heads:
            a = masks_ref[HG_LEVELS] * _dot_nt(qq[:, cs].astype(BF16), kk[:, cs].astype(BF16))
            for lev in range(HG_LEVELS):
                e = ch["e_level"][lev][:, cs]
                a = a + masks_ref[lev] * _dot_nt((qq[:, cs] * e).astype(BF16),
                                                 (kk[:, cs] * e).astype(BF16))
            attn.append(a)
        return attn

    def carry_state(ch, attn):
        rows, e_cum = ch["rows"], ch["e_cum"]
        carried = [_dot_nt((ch["qq"][:, cs] * e_cum[:, cs]).astype(BF16),
                           state_ref[h].astype(BF16))
                   for h, cs in enumerate(heads)]
        update = [_dot_tn(i_ref[rows, cs], (ch["kk"][:, cs] * ch["e_tail"][:, cs]).astype(BF16))
                  for cs in heads]
        for h, cs in enumerate(heads):
            o = _dot(attn[h].astype(BF16), i_ref[rows, cs]) + carried[h]
            state_ref[h] = state_ref[h] * e_cum[c - 1:c, cs] + update[h]
            o_ref[rows, cs] = (_rms(o, outg_ref[...]) * ch["gate"][:, cs]).astype(BF16)

    def chunk_pair(cp, _):
        chunks = [gates_and_decays(2 * cp + u) for u in range(2)]
        attn = [intra_chunk(ch) for ch in chunks]
        for ch, a in zip(chunks, attn):
            carry_state(ch, a)
        return 0

    lax.fori_loop(0, HG_T // (2 * c), chunk_pair, 0)


def _hgrn2(proj3, lower_bounds, out_g, layer):
    b, s, _ = proj3.shape
    sums, masks = _hgrn_constants()

    def col(cblk):
        return pl.BlockSpec((None, HG_T, MIX_W), lambda bi, t: (bi, t, cblk))

    def whole(shape):
        return pl.BlockSpec(shape, lambda bi, t: (0,) * len(shape))

    return pl.pallas_call(
        functools.partial(_hgrn_body, layer=layer),
        grid=(b, s // HG_T),
        in_specs=[col(COL_HG_Q), col(COL_HG_F), col(COL_HG_I), col(COL_HG_G),
                  whole(lower_bounds.shape), whole((1, HEAD_W)),
                  whole(sums.shape), whole(masks.shape)],
        out_specs=pl.BlockSpec((None, HG_T, MIX_W), lambda bi, t: (bi, t, 0)),
        out_shape=jax.ShapeDtypeStruct((b, s, MIX_W), BF16),
        scratch_shapes=[pltpu.VMEM((HG_HEADS, HEAD_W, HEAD_W), F32)],
        compiler_params=pltpu.CompilerParams(
            dimension_semantics=("parallel", "arbitrary"), vmem_limit_bytes=VMEM_LIMIT),
        name="hgrn2",
    )(proj3, proj3, proj3, proj3, lower_bounds, out_g.reshape(1, HEAD_W),
      jnp.asarray(sums, BF16), jnp.asarray(masks, F32))


CA_TQ = 2048


def _cross_attn_body(q_ref, mk_ref, mv_ref, o_ref):
    heads = [slice(h * HEAD_W, (h + 1) * HEAD_W) for h in range(CA_HEADS)]
    scores = _dot_nt(q_ref[:, heads[0]], mk_ref[:, heads[0]])
    for h, cs in enumerate(heads):
        s = scores * (HEAD_W ** -0.5)
        if h + 1 < CA_HEADS:
            scores = _dot_nt(q_ref[:, heads[h + 1]], mk_ref[:, heads[h + 1]])
        p = jnp.exp(s - jnp.max(s, axis=1, keepdims=True))
        l = jnp.sum(p, axis=1, keepdims=True)
        o_ref[:, cs] = (_dot(p.astype(BF16), mv_ref[:, cs]) / l).astype(BF16)


def _cross_attention(proj3, mkv3):
    b, s, _ = proj3.shape
    m = mkv3.shape[1]
    return pl.pallas_call(
        _cross_attn_body,
        grid=(b, s // CA_TQ),
        in_specs=[
            pl.BlockSpec((None, CA_TQ, MIX_W), lambda bi, i: (bi, i, COL_CA_Q)),
            pl.BlockSpec((None, m, MIX_W), lambda bi, i: (bi, 0, 0)),
            pl.BlockSpec((None, m, MIX_W), lambda bi, i: (bi, 0, 1)),
        ],
        out_specs=pl.BlockSpec((None, CA_TQ, MIX_W), lambda bi, i: (bi, i, 0)),
        out_shape=jax.ShapeDtypeStruct((b, s, MIX_W), BF16),
        compiler_params=pltpu.CompilerParams(
            dimension_semantics=("parallel", "parallel"), vmem_limit_bytes=VMEM_LIMIT),
        name="cross_attention",
    )(proj3, mkv3, mkv3)


MERGE_TM = 512
ROUTE_W = 128


def _top2_of_4(r0, r1, r2, r3):
    hi1, lo1 = jnp.maximum(r0, r1), jnp.minimum(r0, r1)
    hi2, lo2 = jnp.maximum(r2, r3), jnp.minimum(r2, r3)
    return jnp.maximum(hi1, hi2), jnp.maximum(jnp.minimum(hi1, hi2), jnp.maximum(lo1, lo2))


def _argmax_first(vals):
    best_v, best_i = vals[0], jnp.zeros_like(vals[0])
    for i in range(1, len(vals)):
        upd = vals[i] > best_v
        best_i = jnp.where(upd, float(i), best_i)
        best_v = jnp.where(upd, vals[i], best_v)
    return best_i


def _pick(idx, vals):
    out = vals[0]
    for i in range(1, len(vals)):
        out = jnp.where(idx == float(i), vals[i], out)
    return out


def _route(logits_t, bias_col):
    scores = jax.nn.sigmoid(logits_t)
    sel = scores + bias_col
    sel_rows = [sel[e:e + 1, :] for e in range(N_EXPERTS)]
    score_rows = [scores[e:e + 1, :] for e in range(N_EXPERTS)]
    grp = []
    for g in range(N_GROUPS):
        m1, m2 = _top2_of_4(*sel_rows[4 * g:4 * g + 4])
        grp.append(m1 + m2)
    best = _argmax_first(grp)
    cand = [_pick(best, [sel_rows[4 * g + i] for g in range(N_GROUPS)])
            for i in range(EXPERTS_PER_GROUP)]
    cand_score = [_pick(best, [score_rows[4 * g + i] for g in range(N_GROUPS)])
                  for i in range(EXPERTS_PER_GROUP)]
    i1 = _argmax_first(cand)
    i2 = _argmax_first([jnp.where(i1 == float(i), NEG_INF, cand[i])
                        for i in range(EXPERTS_PER_GROUP)])
    w1, w2 = _pick(i1, cand_score), _pick(i2, cand_score)
    tot = w1 + w2
    return w1 / tot, w2 / tot, best * 4.0 + i1, best * 4.0 + i2


def _merge_body(ya_ref, yb_ref, yc_ref, ga_ref, gb_ref, gc_ref, x_ref, wb_ref, wo_ref,
                gffn_ref, wr_hi_ref, wr_lo_ref, rbias_ref, tri_ref,
                x1_ref, h2_ref, route_ref, route_rows_ref, counts_ref):
    halves = [slice(0, MERGE_TM // 2), slice(MERGE_TM // 2, MERGE_TM)]
    mixers = ((ya_ref, ga_ref), (yb_ref, gb_ref), (yc_ref, gc_ref))
    branch = [[_dot(y_ref[rows, :], wb_ref[i]) for i, (y_ref, _) in enumerate(mixers)]
              for rows in halves]
    merged = [sum(jax.nn.sigmoid(gate_ref[rows, :].astype(F32)) * branch[hf][i]
                  for i, (_, gate_ref) in enumerate(mixers))
              for hf, rows in enumerate(halves)]
    x1 = [x_ref[rows, :] + _dot(merged[hf].astype(BF16), wo_ref[...])
          for hf, rows in enumerate(halves)]
    logits = []
    for hf, rows in enumerate(halves):
        x1_ref[rows, :] = x1[hf]
        h2 = _rms(x1[hf], gffn_ref[...])
        h2_ref[rows, :] = h2.astype(BF16)
        h_hi, h_lo = _split_bf16(h2)
        logits.append(_dot(h_hi, wr_hi_ref[...])
                      + (_dot(h_hi, wr_lo_ref[...]) + _dot(h_lo, wr_hi_ref[...])))
    logits_t = jnp.concatenate(logits, axis=0).T[0:N_EXPERTS, :]
    w1, w2, e1, e2 = _route(logits_t, rbias_ref[...])
    expert = lax.broadcasted_iota(jnp.int32, (N_EXPERTS, MERGE_TM), 0).astype(F32)
    pick1 = jnp.where(expert == e1, 1.0, 0.0)
    pick2 = jnp.where(expert == e2, 1.0, 0.0)
    picked = pick1 + pick2
    before = _dot(picked.astype(BF16), tri_ref[...])
    r1 = jnp.sum(pick1 * before, axis=0, keepdims=True)
    r2 = jnp.sum(pick2 * before, axis=0, keepdims=True)
    counts_ref[...] = jnp.broadcast_to(jnp.sum(picked, axis=1, keepdims=True),
                                       counts_ref.shape)
    fields = (w1, w2, e1, e2, r1, r2)
    row = lax.broadcasted_iota(jnp.int32, (ROUTE_W, MERGE_TM), 0)
    packed = jnp.zeros((ROUTE_W, MERGE_TM), F32)
    for i, f in enumerate(fields):
        packed = jnp.where(row == i, f, packed)
    route_ref[...] = packed.T
    route_rows_ref[...] = packed[0:8, :]


def _merge(ya, yb, yc, proj, x2d, w_branch, w_out, g_ffn, wr_hi, wr_lo, rbias):
    n = x2d.shape[0]
    tm = MERGE_TM

    def rows(width, cblk=0):
        return pl.BlockSpec((tm, width), lambda i: (i, cblk))

    def whole(shape):
        return pl.BlockSpec(shape, lambda i: (0,) * len(shape), pipeline_mode=pl.Buffered(1))

    return pl.pallas_call(
        _merge_body,
        grid=(n // tm,),
        in_specs=[rows(MIX_W), rows(MIX_W), rows(MIX_W),
                  rows(D_MODEL, COL_GATES // D_MODEL), rows(D_MODEL, COL_GATES // D_MODEL + 1),
                  rows(D_MODEL, COL_GATES // D_MODEL + 2),
                  rows(D_MODEL),
                  whole(w_branch.shape), whole(w_out.shape), whole((1, D_MODEL)),
                  whole(wr_hi.shape), whole(wr_lo.shape), whole((N_EXPERTS, 1)),
                  whole((tm, tm))],
        out_specs=[rows(D_MODEL), rows(D_MODEL), rows(ROUTE_W),
                   pl.BlockSpec((None, 8, tm), lambda i: (i, 0, 0)),
                   pl.BlockSpec((None, N_EXPERTS, ROUTE_W), lambda i: (i, 0, 0))],
        out_shape=[jax.ShapeDtypeStruct((n, D_MODEL), F32),
                   jax.ShapeDtypeStruct((n, D_MODEL), BF16),
                   jax.ShapeDtypeStruct((n, ROUTE_W), F32),
                   jax.ShapeDtypeStruct((n // tm, 8, tm), F32),
                   jax.ShapeDtypeStruct((n // tm, N_EXPERTS, ROUTE_W), F32)],
        compiler_params=pltpu.CompilerParams(
            dimension_semantics=("parallel",), vmem_limit_bytes=VMEM_LIMIT),
        name="merge_route",
    )(ya, yb, yc, proj, proj, proj, x2d, w_branch, w_out, g_ffn.reshape(1, D_MODEL), wr_hi, wr_lo,
      rbias.reshape(N_EXPERTS, 1),
      jnp.asarray(np.triu(np.ones((tm, tm), np.float32), 1), BF16))


PIECE = 16
BLOCK_ROWS = -(-(2 * MERGE_TM + N_EXPERTS * (PIECE - 1)) // 256) * 256
BLOCK_PIECES = BLOCK_ROWS // PIECE
FFN_TM = 512
XS_W = D_MODEL + 128
TAIL_PIECES = FFN_TM // PIECE


def _max_ffn_tiles(n_blocks):
    rows = n_blocks * (2 * MERGE_TM + N_EXPERTS * (PIECE - 1)) + N_EXPERTS * (FFN_TM - PIECE)
    return -(-rows // FFN_TM)


def _dispatch_plan(counts_out):
    i32 = jnp.int32
    counts = counts_out[:, :, 0].astype(i32)
    n_blocks = counts.shape[0]
    padded = (counts + PIECE - 1) // PIECE * PIECE
    loc = jnp.cumsum(padded, axis=1) - padded
    tot = jnp.sum(padded, axis=0)
    region = (tot + FFN_TM - 1) // FFN_TM * FFN_TM
    off = jnp.cumsum(region) - region
    seg = off[None, :] + jnp.cumsum(padded, axis=0) - padded
    experts = jnp.arange(N_EXPERTS, dtype=i32)

    def pick(table, e_idx):
        return jnp.sum(jnp.where(e_idx[..., None] == experts, table, 0), axis=-1, dtype=i32)

    k_row = jnp.arange(BLOCK_PIECES, dtype=i32) * PIECE
    e_of = jnp.sum(k_row[None, :, None] >= (loc + padded)[:, None, :], axis=2, dtype=i32)
    dest = pick((seg - loc)[:, None, :], jnp.minimum(e_of, N_EXPERTS - 1)) + k_row[None, :]
    n_tail = (region - tot) // PIECE
    tail_end = jnp.cumsum(n_tail)
    j = jnp.arange(N_EXPERTS * TAIL_PIECES, dtype=i32)
    e_tail = jnp.minimum(jnp.sum(j[:, None] >= tail_end[None, :], axis=1, dtype=i32),
                         N_EXPERTS - 1)
    zero_dest = pick((off + tot - (tail_end - n_tail) * PIECE)[None, :], e_tail) + j * PIECE
    tile_row = jnp.arange(_max_ffn_tiles(n_blocks), dtype=i32) * FFN_TM
    tile_expert = jnp.sum(tile_row[:, None] >= (off + region)[None, :], axis=1, dtype=i32)
    return dict(
        loc=loc.reshape(-1), dest=dest.reshape(-1),
        n_pieces=(jnp.sum(padded, axis=1) // PIECE).astype(i32),
        zero_dest=zero_dest, n_zero=tail_end[-1:].astype(i32),
        tile_expert=jnp.minimum(tile_expert, N_EXPERTS - 1),
        n_tiles=(jnp.sum(region) // FFN_TM).astype(i32).reshape(1))


def _block_row(expert, rank, loc_ref, block):
    start = jnp.zeros_like(rank)
    for e in range(N_EXPERTS):
        start = jnp.where(expert == float(e), loc_ref[block * N_EXPERTS + e].astype(F32), start)
    return start + rank


def _row_tags(cols):
    lanes = []
    for w_col in (cols[:, 0:1], cols[:, 1:2]):
        hi = w_col.astype(BF16).astype(F32)
        mid = (w_col - hi).astype(BF16).astype(F32)
        lanes += [hi, mid, w_col - hi - mid]
    lanes += [cols[:, 2:3], cols[:, 3:4]]
    lane = lax.broadcasted_iota(jnp.int32, (cols.shape[0], 128), 1)
    tags = jnp.zeros((cols.shape[0], 128), F32)
    for i, v in enumerate(lanes):
        tags = jnp.where(lane == i, v, tags)
    return tags.astype(BF16)


def _for_each(n, fn):
    lax.fori_loop(0, n, lambda k, c: (fn(k), c)[1], 0)


def _dispatch_body(loc_ref, dest_ref, npieces_ref, zdest_ref, nzero_ref, ntiles_ref,
                   rows_ref, cols_ref, h_ref, xs_hbm, buf_ref, zero_ref, sem, zero_sem):
    b = pl.program_id(0)
    slot = b % 2

    def piece_copy(k, block, s):
        src = buf_ref.at[s, pl.ds(pl.multiple_of(k * PIECE, PIECE), PIECE), :]
        dst = xs_hbm.at[pl.ds(pl.multiple_of(dest_ref[block * BLOCK_PIECES + k], PIECE), PIECE), :]
        return pltpu.make_async_copy(src, dst, sem.at[s])

    def zero_copy(j):
        dst = xs_hbm.at[pl.ds(pl.multiple_of(zdest_ref[j], PIECE), PIECE), :]
        return pltpu.make_async_copy(zero_ref.at[pl.ds(0, PIECE), :], dst, zero_sem)

    def zero_tile_copy(t):
        dst = xs_hbm.at[pl.ds(pl.multiple_of(t * FFN_TM, FFN_TM), FFN_TM), :]
        return pltpu.make_async_copy(zero_ref, dst, zero_sem)

    idle_tiles = xs_hbm.shape[0] // FFN_TM - ntiles_ref[0]

    @pl.when(b == 0)
    def _start_zero_fill():
        zero_ref[...] = jnp.zeros_like(zero_ref)
        _for_each(nzero_ref[0], lambda j: zero_copy(j).start())
        _for_each(idle_tiles, lambda t: zero_tile_copy(ntiles_ref[0] + t).start())

    idx1 = _block_row(rows_ref[2:3, :], rows_ref[4:5, :], loc_ref, b)
    idx2 = _block_row(rows_ref[3:4, :], rows_ref[5:6, :], loc_ref, b)
    row = lax.broadcasted_iota(jnp.int32, (BLOCK_ROWS, MERGE_TM), 0).astype(F32)
    permute = jnp.where((row == idx1) | (row == idx2), 1.0, 0.0).astype(BF16)
    buf_ref[slot, :, 0:D_MODEL] = _dot(permute, h_ref[...]).astype(BF16)
    buf_ref[slot, :, D_MODEL:XS_W] = _dot(permute, _row_tags(cols_ref[...])).astype(BF16)
    _for_each(npieces_ref[b], lambda k: piece_copy(k, b, slot).start())

    def wait_block(block, s):
        pltpu.make_async_copy(buf_ref.at[s, pl.ds(0, 2 * MERGE_TM), :],
                              xs_hbm.at[pl.ds(0, 2 * MERGE_TM), :], sem.at[s]).wait()
        _for_each(npieces_ref[block] - 2 * MERGE_TM // PIECE,
                  lambda k: piece_copy(k, block, s).wait())

    @pl.when(b > 0)
    def _wait_previous_block():
        wait_block(b - 1, 1 - slot)

    @pl.when(b == pl.num_programs(0) - 1)
    def _wait_last_block():
        wait_block(b, slot)

    @pl.when(b == 0)
    def _wait_zero_fill():
        _for_each(nzero_ref[0], lambda j: zero_copy(j).wait())
        _for_each(idle_tiles, lambda t: zero_tile_copy(ntiles_ref[0] + t).wait())


def _dispatch(plan, route_rows, route, h2):
    n_blocks = route_rows.shape[0]
    rows_max = _max_ffn_tiles(n_blocks) * FFN_TM
    return pl.pallas_call(
        _dispatch_body,
        grid_spec=pltpu.PrefetchScalarGridSpec(
            num_scalar_prefetch=6,
            grid=(n_blocks,),
            in_specs=[pl.BlockSpec((None, 8, MERGE_TM), lambda i, *_: (i, 0, 0)),
                      pl.BlockSpec((MERGE_TM, ROUTE_W), lambda i, *_: (i, 0)),
                      pl.BlockSpec((MERGE_TM, D_MODEL), lambda i, *_: (i, 0))],
            out_specs=pl.BlockSpec(memory_space=pl.ANY),
            scratch_shapes=[pltpu.VMEM((2, BLOCK_ROWS, XS_W), BF16),
                            pltpu.VMEM((FFN_TM, XS_W), BF16),
                            pltpu.SemaphoreType.DMA((2,)),
                            pltpu.SemaphoreType.DMA(())]),
        out_shape=jax.ShapeDtypeStruct((rows_max, XS_W), BF16),
        compiler_params=pltpu.CompilerParams(
            dimension_semantics=("arbitrary",), vmem_limit_bytes=VMEM_LIMIT),
        name="moe_dispatch",
    )(plan["loc"], plan["dest"], plan["n_pieces"], plan["zero_dest"], plan["n_zero"],
      plan["n_tiles"], route_rows, route, h2)


def _ffn_body(tile_expert_ref, ntiles_ref, xs_ref, wg_ref, wu_ref, wd_ref, ys_ref,
              wg_bf_ref, wu_bf_ref, wd_bf_ref):
    i = pl.program_id(0)
    active = i < ntiles_ref[0]
    expert = tile_expert_ref[i]

    @pl.when(jnp.logical_not(active))
    def _idle_tile():
        ys_ref[...] = jnp.zeros_like(ys_ref)

    @pl.when(active & ((i == 0) | (expert != tile_expert_ref[jnp.maximum(i - 1, 0)])))
    def _cast_expert_weights():
        wg_bf_ref[...] = wg_ref[...].astype(BF16)
        wu_bf_ref[...] = wu_ref[...].astype(BF16)
        wd_bf_ref[...] = wd_ref[...].astype(BF16)

    @pl.when(active)
    def _run_tile():
        halves = [slice(0, FFN_TM // 2), slice(FFN_TM // 2, FFN_TM)]
        gate_up = [(_dot(xs_ref[rows, 0:D_MODEL], wg_bf_ref[...]),
                    _dot(xs_ref[rows, 0:D_MODEL], wu_bf_ref[...])) for rows in halves]
        for rows, (g, u) in zip(halves, gate_up):
            tags = xs_ref[rows, D_MODEL:XS_W].astype(F32)
            first_pick = tags[:, 6:7] == expert.astype(F32)
            weight = jnp.where(first_pick, tags[:, 0:1] + tags[:, 1:2] + tags[:, 2:3],
                               tags[:, 3:4] + tags[:, 4:5] + tags[:, 5:6])
            a = jax.nn.silu(g) * u * weight
            ys_ref[rows, :] = _dot(a.astype(BF16), wd_bf_ref[...]).astype(BF16)


def _expert_ffn(plan, xs, wg, wu, wd, layer):
    n_tiles_max = xs.shape[0] // FFN_TM

    def tile(i, tile_expert, n_tiles):
        return jnp.minimum(i, n_tiles[0] - 1)

    def expert(shape):
        return pl.BlockSpec((None, None) + shape,
                            lambda i, te, nt: (layer, te[tile(i, te, nt)], 0, 0))

    return pl.pallas_call(
        _ffn_body,
        grid_spec=pltpu.PrefetchScalarGridSpec(
            num_scalar_prefetch=2,
            grid=(n_tiles_max,),
            in_specs=[pl.BlockSpec((FFN_TM, XS_W), lambda i, te, nt: (tile(i, te, nt), 0)),
                      expert((D_MODEL, D_EXPERT)), expert((D_MODEL, D_EXPERT)),
                      expert((D_EXPERT, D_MODEL))],
            out_specs=pl.BlockSpec((FFN_TM, D_MODEL), lambda i, te, nt: (i, 0)),
            scratch_shapes=[pltpu.VMEM((D_MODEL, D_EXPERT), BF16),
                            pltpu.VMEM((D_MODEL, D_EXPERT), BF16),
                            pltpu.VMEM((D_EXPERT, D_MODEL), BF16)]),
        out_shape=jax.ShapeDtypeStruct((xs.shape[0], D_MODEL), BF16),
        compiler_params=pltpu.CompilerParams(
            dimension_semantics=("arbitrary",), vmem_limit_bytes=VMEM_LIMIT),
        name="moe_expert_ffn",
    )(plan["tile_expert"], plan["n_tiles"], xs, wg, wu, wd)


def _combine_body(loc_ref, dest_ref, npieces_ref, cols_ref, x1_ref, gfin_ref, ys_hbm, o_ref,
                  buf_ref, sem, *, final_norm):
    b = pl.program_id(0)
    slot = b % 2

    def piece_copy(k, block, s):
        src = ys_hbm.at[pl.ds(pl.multiple_of(dest_ref[block * BLOCK_PIECES + k], PIECE), PIECE), :]
        dst = buf_ref.at[s, pl.ds(pl.multiple_of(k * PIECE, PIECE), PIECE), :]
        return pltpu.make_async_copy(src, dst, sem.at[s])

    @pl.when(b == 0)
    def _first_block():
        buf_ref[...] = jnp.zeros_like(buf_ref)
        _for_each(npieces_ref[0], lambda k: piece_copy(k, 0, 0).start())

    @pl.when(b + 1 < pl.num_programs(0))
    def _prefetch_next_block():
        _for_each(npieces_ref[b + 1], lambda k: piece_copy(k, b + 1, 1 - slot).start())

    pltpu.make_async_copy(ys_hbm.at[pl.ds(0, 2 * MERGE_TM), :],
                          buf_ref.at[slot, pl.ds(0, 2 * MERGE_TM), :], sem.at[slot]).wait()
    _for_each(npieces_ref[b] - 2 * MERGE_TM // PIECE, lambda k: piece_copy(k, b, slot).wait())

    idx1 = _block_row(cols_ref[:, 2:3], cols_ref[:, 4:5], loc_ref, b)
    idx2 = _block_row(cols_ref[:, 3:4], cols_ref[:, 5:6], loc_ref, b)
    col = lax.broadcasted_iota(jnp.int32, (MERGE_TM, BLOCK_ROWS), 1).astype(F32)
    unpermute = jnp.where((col == idx1) | (col == idx2), 1.0, 0.0).astype(BF16)
    out = x1_ref[...] + _dot(unpermute, buf_ref[slot])
    o_ref[...] = _rms(out, gfin_ref[...]) if final_norm else out


def _combine(plan, route, x1, ys, g_final, final_norm):
    n = x1.shape[0]
    return pl.pallas_call(
        functools.partial(_combine_body, final_norm=final_norm),
        grid_spec=pltpu.PrefetchScalarGridSpec(
            num_scalar_prefetch=3,
            grid=(n // MERGE_TM,),
            in_specs=[pl.BlockSpec((MERGE_TM, ROUTE_W), lambda i, *_: (i, 0)),
                      pl.BlockSpec((MERGE_TM, D_MODEL), lambda i, *_: (i, 0)),
                      pl.BlockSpec((1, D_MODEL), lambda i, *_: (0, 0)),
                      pl.BlockSpec(memory_space=pl.ANY)],
            out_specs=pl.BlockSpec((MERGE_TM, D_MODEL), lambda i, *_: (i, 0)),
            scratch_shapes=[pltpu.VMEM((2, BLOCK_ROWS, D_MODEL), BF16),
                            pltpu.SemaphoreType.DMA((2,))]),
        out_shape=jax.ShapeDtypeStruct((n, D_MODEL), F32),
        compiler_params=pltpu.CompilerParams(
            dimension_semantics=("arbitrary",), vmem_limit_bytes=VMEM_LIMIT),
        name="moe_combine",
    )(plan["loc"], plan["dest"], plan["n_pieces"], route, x1, g_final.reshape(1, D_MODEL), ys)


def kernel(x, mem, g_mix, w_in, da_lambda, da_sub_g, hg_lower_bounds, hg_out_g, g_mem, w_mem_kv, w_branch, w_out, g_ffn, w_router, router_bias, w_exp_gate, w_exp_up, w_exp_down, g_final):
    b, s, d = x.shape
    m = mem.shape[1]
    n = b * s
    wr_pad = jnp.pad(w_router.astype(F32), ((0, 0), (0, ROUTE_W - N_EXPERTS)))
    wr_hi, wr_lo = _split_bf16(wr_pad)
    x2d = x.reshape(n, d)
    mem2d = mem.reshape(b * m, d)
    for l in range(DEPTH):
        lam_init = 0.8 - 0.6 * math.exp(-0.3 * l)
        proj = _norm_proj(x2d, g_mix[l], w_in[l].astype(BF16), row_tile=512, name="in_proj")
        proj3 = proj.reshape(b, s, IN_TOTAL)
        mkv = _norm_proj(mem2d, g_mem[l], w_mem_kv[l].astype(BF16), row_tile=512,
                         name="mem_kv_proj")
        y_a = _diff_attention(proj3, da_lambda[l].astype(F32), da_sub_g[l], lam_init)
        y_b = _hgrn2(proj3, hg_lower_bounds.astype(F32), hg_out_g[l], l)
        y_c = _cross_attention(proj3, mkv.reshape(b, m, 2 * MIX_W))
        x1, h2, route, route_rows, counts = _merge(
            y_a.reshape(n, MIX_W), y_b.reshape(n, MIX_W), y_c.reshape(n, MIX_W), proj, x2d,
            w_branch[l].astype(BF16), w_out[l].astype(BF16), g_ffn[l],
            wr_hi, wr_lo, router_bias.astype(F32))
        plan = _dispatch_plan(counts)
        xs = _dispatch(plan, route_rows, route, h2)
        ys = _expert_ffn(plan, xs, w_exp_gate, w_exp_up, w_exp_down, l)
        x2d = _combine(plan, route, x1, ys, g_final, final_norm=(l == DEPTH - 1))
    return x2d.reshape(b, s, d)
```

```python
import functools
import math

import numpy as np
import jax
import jax.numpy as jnp
from jax import lax
from jax.experimental import pallas as pl
from jax.experimental.pallas import tpu as pltpu

F32 = jnp.float32
BF16 = jnp.bfloat16

D_MODEL = 1024
DEPTH = 2
DA_HEADS = 4
DA_QK_DIM = 64
HEAD_W = 128
MIX_W = 512
HG_HEADS = 4
HG_CHUNK = 64
CA_HEADS = 4
N_BRANCH = 3
IN_TOTAL = 8 * MIX_W + N_BRANCH * D_MODEL
N_EXPERTS = 16
N_GROUPS = 4
EXPERTS_PER_GROUP = 4
D_EXPERT = 512
EPS = 1e-6

COL_DA_Q, COL_DA_K, COL_DA_V, COL_HG_Q, COL_HG_F, COL_HG_I, COL_HG_G, COL_CA_Q = range(8)
COL_GATES = 8 * MIX_W

VMEM_LIMIT = 56 * 1024 * 1024

NEG_INF = float("-inf")


def _dot(a, b):
    return jnp.dot(a, b, preferred_element_type=F32)


def _dot_nt(a, b):
    return lax.dot_general(a, b, (((1,), (1,)), ((), ())), preferred_element_type=F32)


def _dot_tn(a, b):
    return lax.dot_general(a, b, (((0,), (0,)), ((), ())), preferred_element_type=F32)


def _rms(x, g):
    return x * lax.rsqrt(jnp.mean(x * x, axis=-1, keepdims=True) + EPS) * g


def _split_bf16(x):
    hi = x.astype(BF16)
    lo = (x - hi.astype(F32)).astype(BF16)
    return hi, lo


def _norm_proj_body(x_ref, g_ref, w_ref, o_ref, *, col_chunk):
    h = _rms(x_ref[...], g_ref[...]).astype(BF16)
    for c in range(o_ref.shape[1] // col_chunk):
        cs = slice(c * col_chunk, (c + 1) * col_chunk)
        o_ref[:, cs] = _dot(h, w_ref[:, cs]).astype(BF16)


def _norm_proj(x2d, g, w_bf16, *, row_tile, name):
    n, d = x2d.shape
    width = w_bf16.shape[1]
    return pl.pallas_call(
        functools.partial(_norm_proj_body, col_chunk=512),
        grid=(n // row_tile,),
        in_specs=[
            pl.BlockSpec((row_tile, d), lambda i: (i, 0)),
            pl.BlockSpec((1, d), lambda i: (0, 0)),
            pl.BlockSpec((d, width), lambda i: (0, 0), pipeline_mode=pl.Buffered(1)),
        ],
        out_specs=pl.BlockSpec((row_tile, width), lambda i: (i, 0)),
        out_shape=jax.ShapeDtypeStruct((n, width), BF16),
        compiler_params=pltpu.CompilerParams(
            dimension_semantics=("parallel",), vmem_limit_bytes=VMEM_LIMIT),
        name=name,
    )(x2d, g.reshape(1, d), w_bf16)


DA_TQ = 512
DA_TK = 256
DA_VROWS = HEAD_W + 16
LOG2E = math.log2(math.e)
DA_QSCALE = DA_QK_DIM ** -0.5 * LOG2E
ALIBI_SLOPES = tuple(2.0 ** (-8.0 * (i + 1) / DA_HEADS) for i in range(DA_HEADS))
assert all(math.frexp(s)[0] == 0.5 for s in ALIBI_SLOPES) and DA_TK <= 256


def _bf16_terms(x, n):
    terms, rest = [], np.float32(x)
    for _ in range(n):
        t = rest.astype(jnp.bfloat16)
        terms.append(float(t))
        rest = np.float32(rest - np.float32(t))
    return tuple(terms)


LOG2E_TERMS = _bf16_terms(LOG2E, 3)


def _own_half(idx, m):
    return idx < DA_QK_DIM if m == 0 else idx >= DA_QK_DIM


def _bias_base(m):
    return DA_QK_DIM * (1 - m)


DA_HEADS_PER_STEP = 4


def _diff_attn_body(q_ref, k_ref, v_ref, lam_ref, subg_ref, o_ref, k0_ref, k1_ref, vt_ref,
                    s_ref, max_ref, acc_ref, *, lam_init):
    qi = pl.program_id(2)
    seq = k_ref.shape[0]
    heads = range(DA_HEADS_PER_STEP)
    head_cols = [slice(hh * HEAD_W, (hh + 1) * HEAD_W) for hh in heads]

    def slope_of(hh):
        h = pl.program_id(1) * DA_HEADS_PER_STEP + hh
        return jnp.where(h == 0, ALIBI_SLOPES[0],
                         jnp.where(h == 1, ALIBI_SLOPES[1],
                                   jnp.where(h == 2, ALIBI_SLOPES[2],
                                             ALIBI_SLOPES[3]))).astype(F32)

    slopes = [slope_of(hh) for hh in heads]

    @pl.when(qi == 0)
    def _prepare_keys_and_values():
        lane = lax.broadcasted_iota(jnp.int32, (seq, HEAD_W), 1)
        row = lax.broadcasted_iota(jnp.int32, (seq, HEAD_W), 0)
        for hh in heads:
            k = k_ref[:, head_cols[hh]].astype(F32)
            local_bias = (row & (DA_TK - 1)).astype(F32) * slopes[hh]
            for m, ka_ref in ((0, k0_ref), (1, k1_ref)):
                base = _bias_base(m)
                slot = (lane >= base) & (lane < base + len(LOG2E_TERMS))
                ka_ref[hh] = jnp.where(_own_half(lane, m), k,
                                       jnp.where(slot, local_bias, 0.0)).astype(BF16)
            for j in range(seq // DA_TK):
                vt_ref[hh, j, 0:HEAD_W, :] = (
                    v_ref[j * DA_TK:(j + 1) * DA_TK, head_cols[hh]].astype(F32).T.astype(BF16))
                vt_ref[hh, j, HEAD_W:DA_VROWS, :] = jnp.ones((DA_VROWS - HEAD_W, DA_TK), BF16)

    feat = lax.broadcasted_iota(jnp.int32, (HEAD_W, DA_TQ), 0)

    def q_aug(qt, m):
        side = jnp.zeros((HEAD_W, DA_TQ), F32)
        for i, c in enumerate(LOG2E_TERMS):
            side = jnp.where(feat == _bias_base(m) + i, c, side)
        return jnp.where(_own_half(feat, m), qt, side).astype(BF16)

    q_maps = []
    for hh in heads:
        qt = (q_ref[:, head_cols[hh]].astype(F32) * DA_QSCALE).T
        q_maps.append((q_aug(qt, 0), q_aug(qt, 1)))

    all_queries = slice(0, DA_TQ)

    def start_scores(j, slot, qs=all_queries):
        ks = pl.ds(pl.multiple_of(j * DA_TK, DA_TK), DA_TK)
        for hh in heads:
            s_ref[hh, slot, 0, :, qs] = _dot(k0_ref[hh, ks, :], q_maps[hh][0][:, qs])
            s_ref[hh, slot, 1, :, qs] = _dot(k1_ref[hh, ks, :], q_maps[hh][1][:, qs])

    def consume(j, slot, mask, qs=all_queries):
        for hh in heads:
            vt = vt_ref[hh, j]
            tile_bias = (slopes[hh] * LOG2E) * (j * DA_TK).astype(F32)
            for m in range(2):
                s = s_ref[hh, slot, m, :, qs]
                if mask is not None:
                    s = jnp.where(mask, s, NEG_INF)
                m_old = max_ref[hh, m, :, qs]
                m_new = jnp.maximum(m_old, jnp.max(s, axis=0, keepdims=True) + tile_bias)
                alpha = jnp.exp2(m_old - m_new)
                p = jnp.exp2(s - (m_new - tile_bias))
                max_ref[hh, m, :, qs] = m_new
                acc_ref[hh, m, :, qs] = (alpha * acc_ref[hh, m, :, qs]
                                         + _dot(vt, p.astype(BF16)))

    max_ref[...] = jnp.full(max_ref.shape, NEG_INF, F32)
    acc_ref[...] = jnp.zeros_like(acc_ref)

    tiles_per_q = DA_TQ // DA_TK
    assert tiles_per_q == 2
    n_full = qi * tiles_per_q
    start_scores(0, 0)

    def full_tile_pair(jj):
        j = jj * 2
        start_scores(j + 1, 1)
        consume(j, 0, None)
        start_scores(j + 2, 0)
        consume(j + 1, 1, None)

    def two_pairs(t, _):
        full_tile_pair(2 * t)
        full_tile_pair(2 * t + 1)
        return 0

    lax.fori_loop(0, qi // 2, two_pairs, 0)

    @pl.when(qi % 2 == 1)
    def _odd_pair():
        full_tile_pair(qi - 1)

    late_queries = slice(DA_TK, DA_TQ)
    rk = lax.broadcasted_iota(jnp.int32, (DA_TK, DA_TQ), 0)
    cq = lax.broadcasted_iota(jnp.int32, (DA_TK, DA_TQ), 1)
    start_scores(n_full + 1, 1, late_queries)
    consume(n_full, 0, cq >= rk)
    consume(n_full + 1, 1, (cq >= rk)[:, 0:DA_TK], late_queries)

    lp = lam_ref[...]
    lam = (jnp.exp(jnp.sum(lp[0:1] * lp[1:2], axis=1, keepdims=True))
           - jnp.exp(jnp.sum(lp[2:3] * lp[3:4], axis=1, keepdims=True)) + lam_init)
    for hh in heads:
        a0, l0 = acc_ref[hh, 0, 0:HEAD_W, :], acc_ref[hh, 0, HEAD_W:HEAD_W + 1, :]
        a1, l1 = acc_ref[hh, 1, 0:HEAD_W, :], acc_ref[hh, 1, HEAD_W:HEAD_W + 1, :]
        o_t = a0 / l0 - lam * (a1 / l1)
        y_t = (o_t * lax.rsqrt(jnp.mean(o_t * o_t, axis=0, keepdims=True) + EPS)
               * subg_ref[...])
        o_ref[:, head_cols[hh]] = (y_t * (1.0 - lam_init)).T.astype(BF16)


def _diff_attention(proj3, lam_params, sub_g, lam_init):
    b, s, _ = proj3.shape
    hb = DA_HEADS_PER_STEP
    width = hb * HEAD_W
    per_stream = MIX_W // width
    return pl.pallas_call(
        functools.partial(_diff_attn_body, lam_init=lam_init),
        grid=(b, DA_HEADS // hb, s // DA_TQ),
        in_specs=[
            pl.BlockSpec((None, DA_TQ, width), lambda bi, g, i: (bi, i, COL_DA_Q * per_stream + g)),
            pl.BlockSpec((None, s, width), lambda bi, g, i: (bi, 0, COL_DA_K * per_stream + g)),
            pl.BlockSpec((None, s, width), lambda bi, g, i: (bi, 0, COL_DA_V * per_stream + g)),
            pl.BlockSpec((4, DA_QK_DIM), lambda bi, g, i: (0, 0)),
            pl.BlockSpec((HEAD_W, 1), lambda bi, g, i: (0, 0)),
        ],
        out_specs=pl.BlockSpec((None, DA_TQ, width), lambda bi, g, i: (bi, i, g)),
        out_shape=jax.ShapeDtypeStruct((b, s, MIX_W), BF16),
        scratch_shapes=[pltpu.VMEM((hb, s, HEAD_W), BF16), pltpu.VMEM((hb, s, HEAD_W), BF16),
                        pltpu.VMEM((hb, s // DA_TK, DA_VROWS, DA_TK), BF16),
                        pltpu.VMEM((hb, 2, 2, DA_TK, DA_TQ), F32),
                        pltpu.VMEM((hb, 2, 1, DA_TQ), F32),
                        pltpu.VMEM((hb, 2, DA_VROWS, DA_TQ), F32)],
        compiler_params=pltpu.CompilerParams(
            dimension_semantics=("parallel", "parallel", "arbitrary"),
            vmem_limit_bytes=VMEM_LIMIT),
        name="diff_attention",
    )(proj3, proj3, proj3, lam_params, sub_g.reshape(HEAD_W, 1))


HG_T = 1024
HG_LEVELS = int(math.log2(HG_CHUNK))
HG_SUBLANES = 8
HG_FINE_LEVELS = tuple(lev for lev in range(HG_LEVELS)
                       if HG_CHUNK >> (lev + 1) < HG_SUBLANES)


def _hgrn_constants():
    c = HG_CHUNK
    r = np.arange(c)
    sums = np.zeros((1 + len(HG_FINE_LEVELS), c, c), np.float32)
    masks = np.zeros((HG_LEVELS + 1, c, c), np.float32)
    sums[0] = (r[None, :] <= r[:, None])
    for lev in range(HG_LEVELS):
        half = c >> (lev + 1)
        mid = (r // (2 * half)) * (2 * half) + half
        second = (r % (2 * half)) >= half
        rp = r[None, :]
        t_rows = second[:, None] & (rp >= mid[:, None]) & (rp <= r[:, None])
        s_rows = (~second)[:, None] & (rp > r[:, None]) & (rp < mid[:, None])
        if lev in HG_FINE_LEVELS:
            sums[1 + HG_FINE_LEVELS.index(lev)] = t_rows | s_rows
        same_block = (r[:, None] // (2 * half)) == (r[None, :] // (2 * half))
        masks[lev] = same_block & second[:, None] & (~second)[None, :]
    masks[HG_LEVELS] = np.eye(c)
    return sums.reshape(-1, c), masks


def _coarse_level_sums(bcum, half):
    parts = []
    for r0 in range(0, HG_CHUNK, HG_SUBLANES):
        mid = r0 // (2 * half) * (2 * half) + half
        ref = bcum[mid - 1:mid, :]
        rows = bcum[r0:r0 + HG_SUBLANES, :]
        parts.append(rows - ref if r0 % (2 * half) >= half else ref - rows)
    return jnp.concatenate(parts, axis=0)


def _layer_lower_bound(raw, layer):
    rows = [raw[i:i + 1] for i in range(raw.shape[0])]
    top = functools.reduce(jnp.maximum, rows)
    ex = [jnp.exp(r - top) for r in rows]
    tot = functools.reduce(jnp.add, ex)
    sm = [e / tot for e in ex]
    return functools.reduce(jnp.add, sm[:layer + 1]) - sm[0]


def _hgrn_body(q_ref, f_ref, i_ref, g_ref, lb_ref, outg_ref, sums_ref, masks_ref, o_ref,
               state_ref, *, layer):
    @pl.when(pl.program_id(1) == 0)
    def _reset_state():
        state_ref[...] = jnp.zeros_like(state_ref)

    c = HG_CHUNK
    lb = _layer_lower_bound(lb_ref[...], layer)
    sums = sums_ref[...]

    heads = [slice(h * HEAD_W, (h + 1) * HEAD_W) for h in range(HG_HEADS)]

    def gates_and_decays(ci):
        rows = pl.ds(pl.multiple_of(ci * c, c), c)
        f = lb + (1.0 - lb) * jax.nn.sigmoid(f_ref[rows, :].astype(F32))
        log_f = jnp.log(f)
        lf_hi, lf_lo = _split_bf16(log_f)
        fine = _dot(sums, lf_hi) + _dot(sums, lf_lo)
        bcum = fine[0:c, :]
        e_level = [jnp.exp(fine[(1 + HG_FINE_LEVELS.index(lev)) * c:
                                (2 + HG_FINE_LEVELS.index(lev)) * c, :])
                   if lev in HG_FINE_LEVELS else jnp.exp(_coarse_level_sums(bcum, c >> (lev + 1)))
                   for lev in range(HG_LEVELS)]
        return dict(rows=rows, kk=1.0 - f, qq=jax.nn.silu(q_ref[rows, :].astype(F32)),
                    e_cum=jnp.exp(bcum), e_tail=jnp.exp(bcum[c - 1:c, :] - bcum),
                    e_level=e_level, gate=jax.nn.silu(g_ref[rows, :].astype(F32)))

    def intra_chunk(ch):
        qq, kk = ch["qq"], ch["kk"]
        attn = []
        for cs in heads:
            a = masks_ref[HG_LEVELS] * _dot_nt(qq[:, cs].astype(BF16), kk[:, cs].astype(BF16))
            for lev in range(HG_LEVELS):
                e = ch["e_level"][lev][:, cs]
                a = a + masks_ref[lev] * _dot_nt((qq[:, cs] * e).astype(BF16),
                                                 (kk[:, cs] * e).astype(BF16))
            attn.append(a)
        return attn

    def carry_state(ch, attn):
        rows, e_cum = ch["rows"], ch["e_cum"]
        carried = [_dot_nt((ch["qq"][:, cs] * e_cum[:, cs]).astype(BF16),
                           state_ref[h].astype(BF16))
                   for h, cs in enumerate(heads)]
        update = [_dot_tn(i_ref[rows, cs], (ch["kk"][:, cs] * ch["e_tail"][:, cs]).astype(BF16))
                  for cs in heads]
        for h, cs in enumerate(heads):
            o = _dot(attn[h].astype(BF16), i_ref[rows, cs]) + carried[h]
            state_ref[h] = state_ref[h] * e_cum[c - 1:c, cs] + update[h]
            o_ref[rows, cs] = (_rms(o, outg_ref[...]) * ch["gate"][:, cs]).astype(BF16)

    def chunk_pair(cp, _):
        chunks = [gates_and_decays(2 * cp + u) for u in range(2)]
        attn = [intra_chunk(ch) for ch in chunks]
        for ch, a in zip(chunks, attn):
            carry_state(ch, a)
        return 0

    lax.fori_loop(0, HG_T // (2 * c), chunk_pair, 0)


def _hgrn2(proj3, lower_bounds, out_g, layer):
    b, s, _ = proj3.shape
    sums, masks = _hgrn_constants()

    def col(cblk):
        return pl.BlockSpec((None, HG_T, MIX_W), lambda bi, t: (bi, t, cblk))

    def whole(shape):
        return pl.BlockSpec(shape, lambda bi, t: (0,) * len(shape))

    return pl.pallas_call(
        functools.partial(_hgrn_body, layer=layer),
        grid=(b, s // HG_T),
        in_specs=[col(COL_HG_Q), col(COL_HG_F), col(COL_HG_I), col(COL_HG_G),
                  whole(lower_bounds.shape), whole((1, HEAD_W)),
                  whole(sums.shape), whole(masks.shape)],
        out_specs=pl.BlockSpec((None, HG_T, MIX_W), lambda bi, t: (bi, t, 0)),
        out_shape=jax.ShapeDtypeStruct((b, s, MIX_W), BF16),
        scratch_shapes=[pltpu.VMEM((HG_HEADS, HEAD_W, HEAD_W), F32)],
        compiler_params=pltpu.CompilerParams(
            dimension_semantics=("parallel", "arbitrary"), vmem_limit_bytes=VMEM_LIMIT),
        name="hgrn2",
    )(proj3, proj3, proj3, proj3, lower_bounds, out_g.reshape(1, HEAD_W),
      jnp.asarray(sums, BF16), jnp.asarray(masks, F32))


CA_TQ = 2048


def _cross_attn_body(q_ref, mk_ref, mv_ref, o_ref):
    heads = [slice(h * HEAD_W, (h + 1) * HEAD_W) for h in range(CA_HEADS)]
    scores = _dot_nt(q_ref[:, heads[0]], mk_ref[:, heads[0]])
    for h, cs in enumerate(heads):
        s = scores * (HEAD_W ** -0.5)
        if h + 1 < CA_HEADS:
            scores = _dot_nt(q_ref[:, heads[h + 1]], mk_ref[:, heads[h + 1]])
        p = jnp.exp(s - jnp.max(s, axis=1, keepdims=True))
        l = jnp.sum(p, axis=1, keepdims=True)
        o_ref[:, cs] = (_dot(p.astype(BF16), mv_ref[:, cs]) / l).astype(BF16)


def _cross_attention(proj3, mkv3):
    b, s, _ = proj3.shape
    m = mkv3.shape[1]
    return pl.pallas_call(
        _cross_attn_body,
        grid=(b, s // CA_TQ),
        in_specs=[
            pl.BlockSpec((None, CA_TQ, MIX_W), lambda bi, i: (bi, i, COL_CA_Q)),
            pl.BlockSpec((None, m, MIX_W), lambda bi, i: (bi, 0, 0)),
            pl.BlockSpec((None, m, MIX_W), lambda bi, i: (bi, 0, 1)),
        ],
        out_specs=pl.BlockSpec((None, CA_TQ, MIX_W), lambda bi, i: (bi, i, 0)),
        out_shape=jax.ShapeDtypeStruct((b, s, MIX_W), BF16),
        compiler_params=pltpu.CompilerParams(
            dimension_semantics=("parallel", "parallel"), vmem_limit_bytes=VMEM_LIMIT),
        name="cross_attention",
    )(proj3, mkv3, mkv3)


MERGE_TM = 512
ROUTE_W = 128


def _top2_of_4(r0, r1, r2, r3):
    hi1, lo1 = jnp.maximum(r0, r1), jnp.minimum(r0, r1)
    hi2, lo2 = jnp.maximum(r2, r3), jnp.minimum(r2, r3)
    return jnp.maximum(hi1, hi2), jnp.maximum(jnp.minimum(hi1, hi2), jnp.maximum(lo1, lo2))


def _argmax_first(vals):
    best_v, best_i = vals[0], jnp.zeros_like(vals[0])
    for i in range(1, len(vals)):
        upd = vals[i] > best_v
        best_i = jnp.where(upd, float(i), best_i)
        best_v = jnp.where(upd, vals[i], best_v)
    return best_i


def _pick(idx, vals):
    out = vals[0]
    for i in range(1, len(vals)):
        out = jnp.where(idx == float(i), vals[i], out)
    return out


def _route(logits_t, bias_col):
    scores = jax.nn.sigmoid(logits_t)
    sel = scores + bias_col
    sel_rows = [sel[e:e + 1, :] for e in range(N_EXPERTS)]
    score_rows = [scores[e:e + 1, :] for e in range(N_EXPERTS)]
    grp = []
    for g in range(N_GROUPS):
        m1, m2 = _top2_of_4(*sel_rows[4 * g:4 * g + 4])
        grp.append(m1 + m2)
    best = _argmax_first(grp)
    cand = [_pick(best, [sel_rows[4 * g + i] for g in range(N_GROUPS)])
            for i in range(EXPERTS_PER_GROUP)]
    cand_score = [_pick(best, [score_rows[4 * g + i] for g in range(N_GROUPS)])
                  for i in range(EXPERTS_PER_GROUP)]
    i1 = _argmax_first(cand)
    i2 = _argmax_first([jnp.where(i1 == float(i), NEG_INF, cand[i])
                        for i in range(EXPERTS_PER_GROUP)])
    w1, w2 = _pick(i1, cand_score), _pick(i2, cand_score)
    tot = w1 + w2
    return w1 / tot, w2 / tot, best * 4.0 + i1, best * 4.0 + i2


def _merge_body(ya_ref, yb_ref, yc_ref, ga_ref, gb_ref, gc_ref, x_ref, wb_ref, wo_ref,
                gffn_ref, wr_hi_ref, wr_lo_ref, rbias_ref, tri_ref,
                x1_ref, h2_ref, route_ref, route_rows_ref, counts_ref):
    halves = [slice(0, MERGE_TM // 2), slice(MERGE_TM // 2, MERGE_TM)]
    mixers = ((ya_ref, ga_ref), (yb_ref, gb_ref), (yc_ref, gc_ref))
    branch = [[_dot(y_ref[rows, :], wb_ref[i]) for i, (y_ref, _) in enumerate(mixers)]
              for rows in halves]
    merged = [sum(jax.nn.sigmoid(gate_ref[rows, :].astype(F32)) * branch[hf][i]
                  for i, (_, gate_ref) in enumerate(mixers))
              for hf, rows in enumerate(halves)]
    x1 = [x_ref[rows, :] + _dot(merged[hf].astype(BF16), wo_ref[...])
          for hf, rows in enumerate(halves)]
    logits = []
    for hf, rows in enumerate(halves):
        x1_ref[rows, :] = x1[hf]
        h2 = _rms(x1[hf], gffn_ref[...])
        h2_ref[rows, :] = h2.astype(BF16)
        h_hi, h_lo = _split_bf16(h2)
        logits.append(_dot(h_hi, wr_hi_ref[...])
                      + (_dot(h_hi, wr_lo_ref[...]) + _dot(h_lo, wr_hi_ref[...])))
    logits_t = jnp.concatenate(logits, axis=0).T[0:N_EXPERTS, :]
    w1, w2, e1, e2 = _route(logits_t, rbias_ref[...])
    expert = lax.broadcasted_iota(jnp.int32, (N_EXPERTS, MERGE_TM), 0).astype(F32)
    pick1 = jnp.where(expert == e1, 1.0, 0.0)
    pick2 = jnp.where(expert == e2, 1.0, 0.0)
    picked = pick1 + pick2
    before = _dot(picked.astype(BF16), tri_ref[...])
    r1 = jnp.sum(pick1 * before, axis=0, keepdims=True)
    r2 = jnp.sum(pick2 * before, axis=0, keepdims=True)
    counts_ref[...] = jnp.broadcast_to(jnp.sum(picked, axis=1, keepdims=True),
                                       counts_ref.shape)
    fields = (w1, w2, e1, e2, r1, r2)
    row = lax.broadcasted_iota(jnp.int32, (ROUTE_W, MERGE_TM), 0)
    packed = jnp.zeros((ROUTE_W, MERGE_TM), F32)
    for i, f in enumerate(fields):
        packed = jnp.where(row == i, f, packed)
    route_ref[...] = packed.T
    route_rows_ref[...] = packed[0:8, :]


def _merge(ya, yb, yc, proj, x2d, w_branch, w_out, g_ffn, wr_hi, wr_lo, rbias):
    n = x2d.shape[0]
    tm = MERGE_TM

    def rows(width, cblk=0):
        return pl.BlockSpec((tm, width), lambda i: (i, cblk))

    def whole(shape):
        return pl.BlockSpec(shape, lambda i: (0,) * len(shape), pipeline_mode=pl.Buffered(1))

    return pl.pallas_call(
        _merge_body,
        grid=(n // tm,),
        in_specs=[rows(MIX_W), rows(MIX_W), rows(MIX_W),
                  rows(D_MODEL, COL_GATES // D_MODEL), rows(D_MODEL, COL_GATES // D_MODEL + 1),
                  rows(D_MODEL, COL_GATES // D_MODEL + 2),
                  rows(D_MODEL),
                  whole(w_branch.shape), whole(w_out.shape), whole((1, D_MODEL)),
                  whole(wr_hi.shape), whole(wr_lo.shape), whole((N_EXPERTS, 1)),
                  whole((tm, tm))],
        out_specs=[rows(D_MODEL), rows(D_MODEL), rows(ROUTE_W),
                   pl.BlockSpec((None, 8, tm), lambda i: (i, 0, 0)),
                   pl.BlockSpec((None, N_EXPERTS, ROUTE_W), lambda i: (i, 0, 0))],
        out_shape=[jax.ShapeDtypeStruct((n, D_MODEL), F32),
                   jax.ShapeDtypeStruct((n, D_MODEL), BF16),
                   jax.ShapeDtypeStruct((n, ROUTE_W), F32),
                   jax.ShapeDtypeStruct((n // tm, 8, tm), F32),
                   jax.ShapeDtypeStruct((n // tm, N_EXPERTS, ROUTE_W), F32)],
        compiler_params=pltpu.CompilerParams(
            dimension_semantics=("parallel",), vmem_limit_bytes=VMEM_LIMIT),
        name="merge_route",
    )(ya, yb, yc, proj, proj, proj, x2d, w_branch, w_out, g_ffn.reshape(1, D_MODEL), wr_hi, wr_lo,
      rbias.reshape(N_EXPERTS, 1),
      jnp.asarray(np.triu(np.ones((tm, tm), np.float32), 1), BF16))


PIECE = 16
BLOCK_ROWS = -(-(2 * MERGE_TM + N_EXPERTS * (PIECE - 1)) // 256) * 256
BLOCK_PIECES = BLOCK_ROWS // PIECE
FFN_TM = 512
XS_W = D_MODEL + 128
TAIL_PIECES = FFN_TM // PIECE


def _max_ffn_tiles(n_blocks):
    rows = n_blocks * (2 * MERGE_TM + N_EXPERTS * (PIECE - 1)) + N_EXPERTS * (FFN_TM - PIECE)
    return -(-rows // FFN_TM)


def _dispatch_plan(counts_out):
    i32 = jnp.int32
    counts = counts_out[:, :, 0].astype(i32)
    n_blocks = counts.shape[0]
    padded = (counts + PIECE - 1) // PIECE * PIECE
    loc = jnp.cumsum(padded, axis=1) - padded
    tot = jnp.sum(padded, axis=0)
    region = (tot + FFN_TM - 1) // FFN_TM * FFN_TM
    off = jnp.cumsum(region) - region
    seg = off[None, :] + jnp.cumsum(padded, axis=0) - padded
    experts = jnp.arange(N_EXPERTS, dtype=i32)

    def pick(table, e_idx):
        return jnp.sum(jnp.where(e_idx[..., None] == experts, table, 0), axis=-1, dtype=i32)

    k_row = jnp.arange(BLOCK_PIECES, dtype=i32) * PIECE
    e_of = jnp.sum(k_row[None, :, None] >= (loc + padded)[:, None, :], axis=2, dtype=i32)
    dest = pick((seg - loc)[:, None, :], jnp.minimum(e_of, N_EXPERTS - 1)) + k_row[None, :]
    n_tail = (region - tot) // PIECE
    tail_end = jnp.cumsum(n_tail)
    j = jnp.arange(N_EXPERTS * TAIL_PIECES, dtype=i32)
    e_tail = jnp.minimum(jnp.sum(j[:, None] >= tail_end[None, :], axis=1, dtype=i32),
                         N_EXPERTS - 1)
    zero_dest = pick((off + tot - (tail_end - n_tail) * PIECE)[None, :], e_tail) + j * PIECE
    tile_row = jnp.arange(_max_ffn_tiles(n_blocks), dtype=i32) * FFN_TM
    tile_expert = jnp.sum(tile_row[:, None] >= (off + region)[None, :], axis=1, dtype=i32)
    return dict(
        loc=loc.reshape(-1), dest=dest.reshape(-1),
        n_pieces=(jnp.sum(padded, axis=1) // PIECE).astype(i32),
        zero_dest=zero_dest, n_zero=tail_end[-1:].astype(i32),
        tile_expert=jnp.minimum(tile_expert, N_EXPERTS - 1),
        n_tiles=(jnp.sum(region) // FFN_TM).astype(i32).reshape(1))


def _block_row(expert, rank, loc_ref, block):
    start = jnp.zeros_like(rank)
    for e in range(N_EXPERTS):
        start = jnp.where(expert == float(e), loc_ref[block * N_EXPERTS + e].astype(F32), start)
    return start + rank


def _row_tags(cols):
    lanes = []
    for w_col in (cols[:, 0:1], cols[:, 1:2]):
        hi = w_col.astype(BF16).astype(F32)
        mid = (w_col - hi).astype(BF16).astype(F32)
        lanes += [hi, mid, w_col - hi - mid]
    lanes += [cols[:, 2:3], cols[:, 3:4]]
    lane = lax.broadcasted_iota(jnp.int32, (cols.shape[0], 128), 1)
    tags = jnp.zeros((cols.shape[0], 128), F32)
    for i, v in enumerate(lanes):
        tags = jnp.where(lane == i, v, tags)
    return tags.astype(BF16)


def _for_each(n, fn, group=1):
    def trip(i, c):
        for u in range(group):
            fn(i * group + u)
        return c

    lax.fori_loop(0, n // group, trip, 0)
    if group > 1:
        lax.fori_loop(n // group * group, n, lambda k, c: (fn(k), c)[1], 0)


def _dispatch_body(loc_ref, dest_ref, npieces_ref, zdest_ref, nzero_ref, ntiles_ref,
                   rows_ref, cols_ref, h_ref, xs_hbm, buf_ref, zero_ref, sem, zero_sem):
    b = pl.program_id(0)
    slot = b % 2

    def piece_copy(k, block, s):
        src = buf_ref.at[s, pl.ds(pl.multiple_of(k * PIECE, PIECE), PIECE), :]
        dst = xs_hbm.at[pl.ds(pl.multiple_of(dest_ref[block * BLOCK_PIECES + k], PIECE), PIECE), :]
        return pltpu.make_async_copy(src, dst, sem.at[s])

    def zero_copy(j):
        dst = xs_hbm.at[pl.ds(pl.multiple_of(zdest_ref[j], PIECE), PIECE), :]
        return pltpu.make_async_copy(zero_ref.at[pl.ds(0, PIECE), :], dst, zero_sem)

    def zero_tile_copy(t):
        dst = xs_hbm.at[pl.ds(pl.multiple_of(t * FFN_TM, FFN_TM), FFN_TM), :]
        return pltpu.make_async_copy(zero_ref, dst, zero_sem)

    idle_tiles = xs_hbm.shape[0] // FFN_TM - ntiles_ref[0]

    @pl.when(b == 0)
    def _start_zero_fill():
        zero_ref[...] = jnp.zeros_like(zero_ref)
        _for_each(nzero_ref[0], lambda j: zero_copy(j).start())
        _for_each(idle_tiles, lambda t: zero_tile_copy(ntiles_ref[0] + t).start())

    idx1 = _block_row(rows_ref[2:3, :], rows_ref[4:5, :], loc_ref, b)
    idx2 = _block_row(rows_ref[3:4, :], rows_ref[5:6, :], loc_ref, b)
    row = lax.broadcasted_iota(jnp.int32, (BLOCK_ROWS, MERGE_TM), 0).astype(F32)
    permute = jnp.where((row == idx1) | (row == idx2), 1.0, 0.0).astype(BF16)
    buf_ref[slot, :, 0:D_MODEL] = _dot(permute, h_ref[...]).astype(BF16)
    buf_ref[slot, :, D_MODEL:XS_W] = _dot(permute, _row_tags(cols_ref[...])).astype(BF16)
    _for_each(npieces_ref[b], lambda k: piece_copy(k, b, slot).start(), group=4)

    def wait_block(block, s):
        pltpu.make_async_copy(buf_ref.at[s, pl.ds(0, 2 * MERGE_TM), :],
                              xs_hbm.at[pl.ds(0, 2 * MERGE_TM), :], sem.at[s]).wait()
        _for_each(npieces_ref[block] - 2 * MERGE_TM // PIECE,
                  lambda k: piece_copy(k, block, s).wait())

    @pl.when(b > 0)
    def _wait_previous_block():
        wait_block(b - 1, 1 - slot)

    @pl.when(b == pl.num_programs(0) - 1)
    def _wait_last_block():
        wait_block(b, slot)

    @pl.when(b == 0)
    def _wait_zero_fill():
        _for_each(nzero_ref[0], lambda j: zero_copy(j).wait())
        _for_each(idle_tiles, lambda t: zero_tile_copy(ntiles_ref[0] + t).wait())


def _dispatch(plan, route_rows, route, h2):
    n_blocks = route_rows.shape[0]
    rows_max = _max_ffn_tiles(n_blocks) * FFN_TM
    return pl.pallas_call(
        _dispatch_body,
        grid_spec=pltpu.PrefetchScalarGridSpec(
            num_scalar_prefetch=6,
            grid=(n_blocks,),
            in_specs=[pl.BlockSpec((None, 8, MERGE_TM), lambda i, *_: (i, 0, 0)),
                      pl.BlockSpec((MERGE_TM, ROUTE_W), lambda i, *_: (i, 0)),
                      pl.BlockSpec((MERGE_TM, D_MODEL), lambda i, *_: (i, 0))],
            out_specs=pl.BlockSpec(memory_space=pl.ANY),
            scratch_shapes=[pltpu.VMEM((2, BLOCK_ROWS, XS_W), BF16),
                            pltpu.VMEM((FFN_TM, XS_W), BF16),
                            pltpu.SemaphoreType.DMA((2,)),
                            pltpu.SemaphoreType.DMA(())]),
        out_shape=jax.ShapeDtypeStruct((rows_max, XS_W), BF16),
        compiler_params=pltpu.CompilerParams(
            dimension_semantics=("arbitrary",), vmem_limit_bytes=VMEM_LIMIT),
        name="moe_dispatch",
    )(plan["loc"], plan["dest"], plan["n_pieces"], plan["zero_dest"], plan["n_zero"],
      plan["n_tiles"], route_rows, route, h2)


def _ffn_body(tile_expert_ref, ntiles_ref, xs_ref, wg_ref, wu_ref, wd_ref, ys_ref,
              wg_bf_ref, wu_bf_ref, wd_bf_ref):
    i = pl.program_id(0)
    active = i < ntiles_ref[0]
    expert = tile_expert_ref[i]

    @pl.when(jnp.logical_not(active))
    def _idle_tile():
        ys_ref[...] = jnp.zeros_like(ys_ref)

    @pl.when(active & ((i == 0) | (expert != tile_expert_ref[jnp.maximum(i - 1, 0)])))
    def _cast_expert_weights():
        wg_bf_ref[...] = wg_ref[...].astype(BF16)
        wu_bf_ref[...] = wu_ref[...].astype(BF16)
        wd_bf_ref[...] = wd_ref[...].astype(BF16)

    @pl.when(active)
    def _run_tile():
        halves = [slice(0, FFN_TM // 2), slice(FFN_TM // 2, FFN_TM)]
        gate_up = [(_dot(xs_ref[rows, 0:D_MODEL], wg_bf_ref[...]),
                    _dot(xs_ref[rows, 0:D_MODEL], wu_bf_ref[...])) for rows in halves]
        for rows, (g, u) in zip(halves, gate_up):
            tags = xs_ref[rows, D_MODEL:XS_W].astype(F32)
            first_pick = tags[:, 6:7] == expert.astype(F32)
            weight = jnp.where(first_pick, tags[:, 0:1] + tags[:, 1:2] + tags[:, 2:3],
                               tags[:, 3:4] + tags[:, 4:5] + tags[:, 5:6])
            a = jax.nn.silu(g) * u * weight
            ys_ref[rows, :] = _dot(a.astype(BF16), wd_bf_ref[...]).astype(BF16)


def _expert_ffn(plan, xs, wg, wu, wd, layer):
    n_tiles_max = xs.shape[0] // FFN_TM

    def tile(i, tile_expert, n_tiles):
        return jnp.minimum(i, n_tiles[0] - 1)

    def expert(shape):
        return pl.BlockSpec((None, None) + shape,
                            lambda i, te, nt: (layer, te[tile(i, te, nt)], 0, 0))

    return pl.pallas_call(
        _ffn_body,
        grid_spec=pltpu.PrefetchScalarGridSpec(
            num_scalar_prefetch=2,
            grid=(n_tiles_max,),
            in_specs=[pl.BlockSpec((FFN_TM, XS_W), lambda i, te, nt: (tile(i, te, nt), 0)),
                      expert((D_MODEL, D_EXPERT)), expert((D_MODEL, D_EXPERT)),
                      expert((D_EXPERT, D_MODEL))],
            out_specs=pl.BlockSpec((FFN_TM, D_MODEL), lambda i, te, nt: (i, 0)),
            scratch_shapes=[pltpu.VMEM((D_MODEL, D_EXPERT), BF16),
                            pltpu.VMEM((D_MODEL, D_EXPERT), BF16),
                            pltpu.VMEM((D_EXPERT, D_MODEL), BF16)]),
        out_shape=jax.ShapeDtypeStruct((xs.shape[0], D_MODEL), BF16),
        compiler_params=pltpu.CompilerParams(
            dimension_semantics=("arbitrary",), vmem_limit_bytes=VMEM_LIMIT),
        name="moe_expert_ffn",
    )(plan["tile_expert"], plan["n_tiles"], xs, wg, wu, wd)


def _combine_body(loc_ref, dest_ref, npieces_ref, cols_ref, x1_ref, gfin_ref, ys_hbm, o_ref,
                  buf_ref, sem, *, final_norm):
    b = pl.program_id(0)
    slot = b % 2

    def piece_copy(k, block, s):
        src = ys_hbm.at[pl.ds(pl.multiple_of(dest_ref[block * BLOCK_PIECES + k], PIECE), PIECE), :]
        dst = buf_ref.at[s, pl.ds(pl.multiple_of(k * PIECE, PIECE), PIECE), :]
        return pltpu.make_async_copy(src, dst, sem.at[s])

    @pl.when(b == 0)
    def _first_block():
        buf_ref[...] = jnp.zeros_like(buf_ref)
        _for_each(npieces_ref[0], lambda k: piece_copy(k, 0, 0).start(), group=4)

    @pl.when(b + 1 < pl.num_programs(0))
    def _prefetch_next_block():
        _for_each(npieces_ref[b + 1], lambda k: piece_copy(k, b + 1, 1 - slot).start(),
                  group=4)

    pltpu.make_async_copy(ys_hbm.at[pl.ds(0, 2 * MERGE_TM), :],
                          buf_ref.at[slot, pl.ds(0, 2 * MERGE_TM), :], sem.at[slot]).wait()
    _for_each(npieces_ref[b] - 2 * MERGE_TM // PIECE, lambda k: piece_copy(k, b, slot).wait())

    idx1 = _block_row(cols_ref[:, 2:3], cols_ref[:, 4:5], loc_ref, b)
    idx2 = _block_row(cols_ref[:, 3:4], cols_ref[:, 5:6], loc_ref, b)
    col = lax.broadcasted_iota(jnp.int32, (MERGE_TM, BLOCK_ROWS), 1).astype(F32)
    unpermute = jnp.where((col == idx1) | (col == idx2), 1.0, 0.0).astype(BF16)
    out = x1_ref[...] + _dot(unpermute, buf_ref[slot])
    o_ref[...] = _rms(out, gfin_ref[...]) if final_norm else out


def _combine(plan, route, x1, ys, g_final, final_norm):
    n = x1.shape[0]
    return pl.pallas_call(
        functools.partial(_combine_body, final_norm=final_norm),
        grid_spec=pltpu.PrefetchScalarGridSpec(
            num_scalar_prefetch=3,
            grid=(n // MERGE_TM,),
            in_specs=[pl.BlockSpec((MERGE_TM, ROUTE_W), lambda i, *_: (i, 0)),
                      pl.BlockSpec((MERGE_TM, D_MODEL), lambda i, *_: (i, 0)),
                      pl.BlockSpec((1, D_MODEL), lambda i, *_: (0, 0)),
                      pl.BlockSpec(memory_space=pl.ANY)],
            out_specs=pl.BlockSpec((MERGE_TM, D_MODEL), lambda i, *_: (i, 0)),
            scratch_shapes=[pltpu.VMEM((2, BLOCK_ROWS, D_MODEL), BF16),
                            pltpu.SemaphoreType.DMA((2,))]),
        out_shape=jax.ShapeDtypeStruct((n, D_MODEL), F32),
        compiler_params=pltpu.CompilerParams(
            dimension_semantics=("arbitrary",), vmem_limit_bytes=VMEM_LIMIT),
        name="moe_combine",
    )(plan["loc"], plan["dest"], plan["n_pieces"], route, x1, g_final.reshape(1, D_MODEL), ys)


def kernel(x, mem, g_mix, w_in, da_lambda, da_sub_g, hg_lower_bounds, hg_out_g, g_mem, w_mem_kv, w_branch, w_out, g_ffn, w_router, router_bias, w_exp_gate, w_exp_up, w_exp_down, g_final):
    b, s, d = x.shape
    m = mem.shape[1]
    n = b * s
    wr_pad = jnp.pad(w_router.astype(F32), ((0, 0), (0, ROUTE_W - N_EXPERTS)))
    wr_hi, wr_lo = _split_bf16(wr_pad)
    x2d = x.reshape(n, d)
    mem2d = mem.reshape(b * m, d)
    for l in range(DEPTH):
        lam_init = 0.8 - 0.6 * math.exp(-0.3 * l)
        proj = _norm_proj(x2d, g_mix[l], w_in[l].astype(BF16), row_tile=512, name="in_proj")
        proj3 = proj.reshape(b, s, IN_TOTAL)
        mkv = _norm_proj(mem2d, g_mem[l], w_mem_kv[l].astype(BF16), row_tile=512,
                         name="mem_kv_proj")
        y_a = _diff_attention(proj3, da_lambda[l].astype(F32), da_sub_g[l], lam_init)
        y_b = _hgrn2(proj3, hg_lower_bounds.astype(F32), hg_out_g[l], l)
        y_c = _cross_attention(proj3, mkv.reshape(b, m, 2 * MIX_W))
        x1, h2, route, route_rows, counts = _merge(
            y_a.reshape(n, MIX_W), y_b.reshape(n, MIX_W), y_c.reshape(n, MIX_W), proj, x2d,
            w_branch[l].astype(BF16), w_out[l].astype(BF16), g_ffn[l],
            wr_hi, wr_lo, router_bias.astype(F32))
        plan = _dispatch_plan(counts)
        xs = _dispatch(plan, route_rows, route, h2)
        ys = _expert_ffn(plan, xs, w_exp_gate, w_exp_up, w_exp_down, l)
        x2d = _combine(plan, route, x1, ys, g_final, final_norm=(l == DEPTH - 1))
    return x2d.reshape(b, s, d)
```

```python
import functools
import math

import numpy as np
import jax
import jax.numpy as jnp
from jax import lax
from jax.experimental import pallas as pl
from jax.experimental.pallas import tpu as pltpu

F32 = jnp.float32
BF16 = jnp.bfloat16

D_MODEL = 1024
DEPTH = 2
DA_HEADS = 4
DA_QK_DIM = 64
HEAD_W = 128
MIX_W = 512
HG_HEADS = 4
HG_CHUNK = 64
CA_HEADS = 4
N_BRANCH = 3
IN_TOTAL = 8 * MIX_W + N_BRANCH * D_MODEL
N_EXPERTS = 16
N_GROUPS = 4
EXPERTS_PER_GROUP = 4
D_EXPERT = 512
EPS = 1e-6

COL_DA_Q, COL_DA_K, COL_DA_V, COL_HG_Q, COL_HG_F, COL_HG_I, COL_HG_G, COL_CA_Q = range(8)
COL_GATES = 8 * MIX_W

VMEM_LIMIT = 56 * 1024 * 1024

NEG_INF = float("-inf")


def _dot(a, b):
    return jnp.dot(a, b, preferred_element_type=F32)


def _dot_nt(a, b):
    return lax.dot_general(a, b, (((1,), (1,)), ((), ())), preferred_element_type=F32)


def _dot_tn(a, b):
    return lax.dot_general(a, b, (((0,), (0,)), ((), ())), preferred_element_type=F32)


def _rms(x, g):
    return x * lax.rsqrt(jnp.mean(x * x, axis=-1, keepdims=True) + EPS) * g


def _split_bf16(x):
    hi = x.astype(BF16)
    lo = (x - hi.astype(F32)).astype(BF16)
    return hi, lo


def _norm_proj_body(x_ref, g_ref, w_ref, o_ref, *, col_chunk):
    h = _rms(x_ref[...], g_ref[...]).astype(BF16)
    for c in range(o_ref.shape[1] // col_chunk):
        cs = slice(c * col_chunk, (c + 1) * col_chunk)
        o_ref[:, cs] = _dot(h, w_ref[:, cs]).astype(BF16)


def _norm_proj(x2d, g, w_bf16, *, row_tile, name):
    n, d = x2d.shape
    width = w_bf16.shape[1]
    return pl.pallas_call(
        functools.partial(_norm_proj_body, col_chunk=512),
        grid=(n // row_tile,),
        in_specs=[
            pl.BlockSpec((row_tile, d), lambda i: (i, 0)),
            pl.BlockSpec((1, d), lambda i: (0, 0)),
            pl.BlockSpec((d, width), lambda i: (0, 0), pipeline_mode=pl.Buffered(1)),
        ],
        out_specs=pl.BlockSpec((row_tile, width), lambda i: (i, 0)),
        out_shape=jax.ShapeDtypeStruct((n, width), BF16),
        compiler_params=pltpu.CompilerParams(
            dimension_semantics=("parallel",), vmem_limit_bytes=VMEM_LIMIT),
        name=name,
    )(x2d, g.reshape(1, d), w_bf16)


DA_TQ = 512
DA_TK = 256
DA_VROWS = HEAD_W + 16
LOG2E = math.log2(math.e)
DA_QSCALE = DA_QK_DIM ** -0.5 * LOG2E
ALIBI_SLOPES = tuple(2.0 ** (-8.0 * (i + 1) / DA_HEADS) for i in range(DA_HEADS))
assert all(math.frexp(s)[0] == 0.5 for s in ALIBI_SLOPES) and DA_TK <= 256


def _bf16_terms(x, n):
    terms, rest = [], np.float32(x)
    for _ in range(n):
        t = rest.astype(jnp.bfloat16)
        terms.append(float(t))
        rest = np.float32(rest - np.float32(t))
    return tuple(terms)


LOG2E_TERMS = _bf16_terms(LOG2E, 3)


def _own_half(idx, m):
    return idx < DA_QK_DIM if m == 0 else idx >= DA_QK_DIM


def _bias_base(m):
    return DA_QK_DIM * (1 - m)


DA_HEADS_PER_STEP = 4


def _diff_attn_body(q_ref, k_ref, v_ref, lam_ref, subg_ref, o_ref, k0_ref, k1_ref, vt_ref,
                    s_ref, max_ref, acc_ref, *, lam_init):
    qi = pl.program_id(2)
    seq = k_ref.shape[0]
    heads = range(DA_HEADS_PER_STEP)
    head_cols = [slice(hh * HEAD_W, (hh + 1) * HEAD_W) for hh in heads]

    def slope_of(hh):
        h = pl.program_id(1) * DA_HEADS_PER_STEP + hh
        return jnp.where(h == 0, ALIBI_SLOPES[0],
                         jnp.where(h == 1, ALIBI_SLOPES[1],
                                   jnp.where(h == 2, ALIBI_SLOPES[2],
                                             ALIBI_SLOPES[3]))).astype(F32)

    slopes = [slope_of(hh) for hh in heads]

    @pl.when(qi == 0)
    def _prepare_keys_and_values():
        lane = lax.broadcasted_iota(jnp.int32, (seq, HEAD_W), 1)
        row = lax.broadcasted_iota(jnp.int32, (seq, HEAD_W), 0)
        for hh in heads:
            k = k_ref[:, head_cols[hh]].astype(F32)
            local_bias = (row & (DA_TK - 1)).astype(F32) * slopes[hh]
            for m, ka_ref in ((0, k0_ref), (1, k1_ref)):
                base = _bias_base(m)
                slot = (lane >= base) & (lane < base + len(LOG2E_TERMS))
                ka_ref[hh] = jnp.where(_own_half(lane, m), k,
                                       jnp.where(slot, local_bias, 0.0)).astype(BF16)
            for j in range(seq // DA_TK):
                vt_ref[hh, j, 0:HEAD_W, :] = (
                    v_ref[j * DA_TK:(j + 1) * DA_TK, head_cols[hh]].astype(F32).T.astype(BF16))
                vt_ref[hh, j, HEAD_W:DA_VROWS, :] = jnp.ones((DA_VROWS - HEAD_W, DA_TK), BF16)

    feat = lax.broadcasted_iota(jnp.int32, (HEAD_W, DA_TQ), 0)

    def q_aug(qt, m):
        side = jnp.zeros((HEAD_W, DA_TQ), F32)
        for i, c in enumerate(LOG2E_TERMS):
            side = jnp.where(feat == _bias_base(m) + i, c, side)
        return jnp.where(_own_half(feat, m), qt, side).astype(BF16)

    q_maps = []
    for hh in heads:
        qt = (q_ref[:, head_cols[hh]].astype(F32) * DA_QSCALE).T
        q_maps.append((q_aug(qt, 0), q_aug(qt, 1)))

    all_queries = slice(0, DA_TQ)

    def start_scores(j, slot, qs=all_queries):
        ks = pl.ds(pl.multiple_of(j * DA_TK, DA_TK), DA_TK)
        for hh in heads:
            s_ref[hh, slot, 0, :, qs] = _dot(k0_ref[hh, ks, :], q_maps[hh][0][:, qs])
            s_ref[hh, slot, 1, :, qs] = _dot(k1_ref[hh, ks, :], q_maps[hh][1][:, qs])

    def consume(j, slot, mask, qs=all_queries):
        for hh in heads:
            vt = vt_ref[hh, j]
            tile_bias = (slopes[hh] * LOG2E) * (j * DA_TK).astype(F32)
            for m in range(2):
                s = s_ref[hh, slot, m, :, qs]
                if mask is not None:
                    s = jnp.where(mask, s, NEG_INF)
                m_old = max_ref[hh, m, :, qs]
                m_new = jnp.maximum(m_old, jnp.max(s, axis=0, keepdims=True) + tile_bias)
                alpha = jnp.exp2(m_old - m_new)
                p = jnp.exp2(s - (m_new - tile_bias))
                max_ref[hh, m, :, qs] = m_new
                acc_ref[hh, m, :, qs] = (alpha * acc_ref[hh, m, :, qs]
                                         + _dot(vt, p.astype(BF16)))

    max_ref[...] = jnp.full(max_ref.shape, NEG_INF, F32)
    acc_ref[...] = jnp.zeros_like(acc_ref)

    tiles_per_q = DA_TQ // DA_TK
    assert tiles_per_q == 2
    n_full = qi * tiles_per_q
    start_scores(0, 0)

    def full_tile_pair(jj):
        j = jj * 2
        start_scores(j + 1, 1)
        consume(j, 0, None)
        start_scores(j + 2, 0)
        consume(j + 1, 1, None)

    def two_pairs(t, _):
        full_tile_pair(2 * t)
        full_tile_pair(2 * t + 1)
        return 0

    lax.fori_loop(0, qi // 2, two_pairs, 0)

    @pl.when(qi % 2 == 1)
    def _odd_pair():
        full_tile_pair(qi - 1)

    late_queries = slice(DA_TK, DA_TQ)
    rk = lax.broadcasted_iota(jnp.int32, (DA_TK, DA_TQ), 0)
    cq = lax.broadcasted_iota(jnp.int32, (DA_TK, DA_TQ), 1)
    start_scores(n_full + 1, 1, late_queries)
    consume(n_full, 0, cq >= rk)
    consume(n_full + 1, 1, (cq >= rk)[:, 0:DA_TK], late_queries)

    lp = lam_ref[...]
    lam = (jnp.exp(jnp.sum(lp[0:1] * lp[1:2], axis=1, keepdims=True))
           - jnp.exp(jnp.sum(lp[2:3] * lp[3:4], axis=1, keepdims=True)) + lam_init)
    for hh in heads:
        a0, l0 = acc_ref[hh, 0, 0:HEAD_W, :], acc_ref[hh, 0, HEAD_W:HEAD_W + 1, :]
        a1, l1 = acc_ref[hh, 1, 0:HEAD_W, :], acc_ref[hh, 1, HEAD_W:HEAD_W + 1, :]
        o_t = a0 / l0 - lam * (a1 / l1)
        y_t = (o_t * lax.rsqrt(jnp.mean(o_t * o_t, axis=0, keepdims=True) + EPS)
               * subg_ref[...])
        o_ref[:, head_cols[hh]] = (y_t * (1.0 - lam_init)).T.astype(BF16)


def _diff_attention(proj3, lam_params, sub_g, lam_init):
    b, s, _ = proj3.shape
    hb = DA_HEADS_PER_STEP
    width = hb * HEAD_W
    per_stream = MIX_W // width
    return pl.pallas_call(
        functools.partial(_diff_attn_body, lam_init=lam_init),
        grid=(b, DA_HEADS // hb, s // DA_TQ),
        in_specs=[
            pl.BlockSpec((None, DA_TQ, width), lambda bi, g, i: (bi, i, COL_DA_Q * per_stream + g)),
            pl.BlockSpec((None, s, width), lambda bi, g, i: (bi, 0, COL_DA_K * per_stream + g)),
            pl.BlockSpec((None, s, width), lambda bi, g, i: (bi, 0, COL_DA_V * per_stream + g)),
            pl.BlockSpec((4, DA_QK_DIM), lambda bi, g, i: (0, 0)),
            pl.BlockSpec((HEAD_W, 1), lambda bi, g, i: (0, 0)),
        ],
        out_specs=pl.BlockSpec((None, DA_TQ, width), lambda bi, g, i: (bi, i, g)),
        out_shape=jax.ShapeDtypeStruct((b, s, MIX_W), BF16),
        scratch_shapes=[pltpu.VMEM((hb, s, HEAD_W), BF16), pltpu.VMEM((hb, s, HEAD_W), BF16),
                        pltpu.VMEM((hb, s // DA_TK, DA_VROWS, DA_TK), BF16),
                        pltpu.VMEM((hb, 2, 2, DA_TK, DA_TQ), F32),
                        pltpu.VMEM((hb, 2, 1, DA_TQ), F32),
                        pltpu.VMEM((hb, 2, DA_VROWS, DA_TQ), F32)],
        compiler_params=pltpu.CompilerParams(
            dimension_semantics=("parallel", "parallel", "arbitrary"),
            vmem_limit_bytes=VMEM_LIMIT),
        name="diff_attention",
    )(proj3, proj3, proj3, lam_params, sub_g.reshape(HEAD_W, 1))


HG_T = 1024
HG_LEVELS = int(math.log2(HG_CHUNK))
HG_SUBLANES = 8
HG_FINE_LEVELS = tuple(lev for lev in range(HG_LEVELS)
                       if HG_CHUNK >> (lev + 1) < HG_SUBLANES)


def _hgrn_constants():
    c = HG_CHUNK
    r = np.arange(c)
    sums = np.zeros((1 + len(HG_FINE_LEVELS), c, c), np.float32)
    masks = np.zeros((HG_LEVELS + 1, c, c), np.float32)
    sums[0] = (r[None, :] <= r[:, None])
    for lev in range(HG_LEVELS):
        half = c >> (lev + 1)
        mid = (r // (2 * half)) * (2 * half) + half
        second = (r % (2 * half)) >= half
        rp = r[None, :]
        t_rows = second[:, None] & (rp >= mid[:, None]) & (rp <= r[:, None])
        s_rows = (~second)[:, None] & (rp > r[:, None]) & (rp < mid[:, None])
        if lev in HG_FINE_LEVELS:
            sums[1 + HG_FINE_LEVELS.index(lev)] = t_rows | s_rows
        same_block = (r[:, None] // (2 * half)) == (r[None, :] // (2 * half))
        masks[lev] = same_block & second[:, None] & (~second)[None, :]
    masks[HG_LEVELS] = np.eye(c)
    return sums.reshape(-1, c), masks


def _coarse_level_sums(bcum, half):
    parts = []
    for r0 in range(0, HG_CHUNK, HG_SUBLANES):
        mid = r0 // (2 * half) * (2 * half) + half
        ref = bcum[mid - 1:mid, :]
        rows = bcum[r0:r0 + HG_SUBLANES, :]
        parts.append(rows - ref if r0 % (2 * half) >= half else ref - rows)
    return jnp.concatenate(parts, axis=0)


def _layer_lower_bound(raw, layer):
    rows = [raw[i:i + 1] for i in range(raw.shape[0])]
    top = functools.reduce(jnp.maximum, rows)
    ex = [jnp.exp(r - top) for r in rows]
    tot = functools.reduce(jnp.add, ex)
    sm = [e / tot for e in ex]
    return functools.reduce(jnp.add, sm[:layer + 1]) - sm[0]


def _hgrn_body(q_ref, f_ref, i_ref, g_ref, lb_ref, outg_ref, sums_ref, masks_ref, o_ref,
               state_ref, *, layer):
    @pl.when(pl.program_id(1) == 0)
    def _reset_state():
        state_ref[...] = jnp.zeros_like(state_ref)

    c = HG_CHUNK
    lb = _layer_lower_bound(lb_ref[...], layer)
    sums = sums_ref[...]

    heads = [slice(h * HEAD_W, (h + 1) * HEAD_W) for h in range(HG_HEADS)]

    def gates_and_decays(ci):
        rows = pl.ds(pl.multiple_of(ci * c, c), c)
        f = lb + (1.0 - lb) * jax.nn.sigmoid(f_ref[rows, :].astype(F32))
        log_f = jnp.log(f)
        lf_hi, lf_lo = _split_bf16(log_f)
        fine = _dot(sums, lf_hi) + _dot(sums, lf_lo)
        bcum = fine[0:c, :]
        e_level = [jnp.exp(fine[(1 + HG_FINE_LEVELS.index(lev)) * c:
                                (2 + HG_FINE_LEVELS.index(lev)) * c, :])
                   if lev in HG_FINE_LEVELS else jnp.exp(_coarse_level_sums(bcum, c >> (lev + 1)))
                   for lev in range(HG_LEVELS)]
        return dict(rows=rows, kk=1.0 - f, qq=jax.nn.silu(q_ref[rows, :].astype(F32)),
                    e_cum=jnp.exp(bcum), e_tail=jnp.exp(bcum[c - 1:c, :] - bcum),
                    e_level=e_level, gate=jax.nn.silu(g_ref[rows, :].astype(F32)))

    def intra_chunk(ch):
        qq, kk = ch["qq"], ch["kk"]
        attn = []
        for cs in heads:
            a = masks_ref[HG_LEVELS] * _dot_nt(qq[:, cs].astype(BF16), kk[:, cs].astype(BF16))
            for lev in range(HG_LEVELS):
                e = ch["e_level"][lev][:, cs]
                a = a + masks_ref[lev] * _dot_nt((qq[:, cs] * e).astype(BF16),
                                                 (kk[:, cs] * e).astype(BF16))
            attn.append(a)
        return attn

    def carry_state(ch, attn):
        rows, e_cum = ch["rows"], ch["e_cum"]
        carried = [_dot_nt((ch["qq"][:, cs] * e_cum[:, cs]).astype(BF16),
                           state_ref[h].astype(BF16))
                   for h, cs in enumerate(heads)]
        update = [_dot_tn(i_ref[rows, cs], (ch["kk"][:, cs] * ch["e_tail"][:, cs]).astype(BF16))
                  for cs in heads]
        for h, cs in enumerate(heads):
            o = _dot(attn[h].astype(BF16), i_ref[rows, cs]) + carried[h]
            state_ref[h] = state_ref[h] * e_cum[c - 1:c, cs] + update[h]
            o_ref[rows, cs] = (_rms(o, outg_ref[...]) * ch["gate"][:, cs]).astype(BF16)

    def chunk_pair(cp):
        chunks = [gates_and_decays(2 * cp + u) for u in range(2)]
        attn = [intra_chunk(ch) for ch in chunks]
        for ch, a in zip(chunks, attn):
            carry_state(ch, a)

    def two_pairs(t, _):
        chunk_pair(2 * t)
        chunk_pair(2 * t + 1)
        return 0

    lax.fori_loop(0, HG_T // (4 * c), two_pairs, 0)


def _hgrn2(proj3, lower_bounds, out_g, layer):
    b, s, _ = proj3.shape
    sums, masks = _hgrn_constants()

    def col(cblk):
        return pl.BlockSpec((None, HG_T, MIX_W), lambda bi, t: (bi, t, cblk))

    def whole(shape):
        return pl.BlockSpec(shape, lambda bi, t: (0,) * len(shape))

    return pl.pallas_call(
        functools.partial(_hgrn_body, layer=layer),
        grid=(b, s // HG_T),
        in_specs=[col(COL_HG_Q), col(COL_HG_F), col(COL_HG_I), col(COL_HG_G),
                  whole(lower_bounds.shape), whole((1, HEAD_W)),
                  whole(sums.shape), whole(masks.shape)],
        out_specs=pl.BlockSpec((None, HG_T, MIX_W), lambda bi, t: (bi, t, 0)),
        out_shape=jax.ShapeDtypeStruct((b, s, MIX_W), BF16),
        scratch_shapes=[pltpu.VMEM((HG_HEADS, HEAD_W, HEAD_W), F32)],
        compiler_params=pltpu.CompilerParams(
            dimension_semantics=("parallel", "arbitrary"), vmem_limit_bytes=VMEM_LIMIT),
        name="hgrn2",
    )(proj3, proj3, proj3, proj3, lower_bounds, out_g.reshape(1, HEAD_W),
      jnp.asarray(sums, BF16), jnp.asarray(masks, F32))


CA_TQ = 2048


def _cross_attn_body(q_ref, mk_ref, mv_ref, o_ref):
    heads = [slice(h * HEAD_W, (h + 1) * HEAD_W) for h in range(CA_HEADS)]
    scores = _dot_nt(q_ref[:, heads[0]], mk_ref[:, heads[0]])
    for h, cs in enumerate(heads):
        s = scores * (HEAD_W ** -0.5)
        if h + 1 < CA_HEADS:
            scores = _dot_nt(q_ref[:, heads[h + 1]], mk_ref[:, heads[h + 1]])
        p = jnp.exp(s - jnp.max(s, axis=1, keepdims=True))
        l = jnp.sum(p, axis=1, keepdims=True)
        o_ref[:, cs] = (_dot(p.astype(BF16), mv_ref[:, cs]) / l).astype(BF16)


def _cross_attention(proj3, mkv3):
    b, s, _ = proj3.shape
    m = mkv3.shape[1]
    return pl.pallas_call(
        _cross_attn_body,
        grid=(b, s // CA_TQ),
        in_specs=[
            pl.BlockSpec((None, CA_TQ, MIX_W), lambda bi, i: (bi, i, COL_CA_Q)),
            pl.BlockSpec((None, m, MIX_W), lambda bi, i: (bi, 0, 0)),
            pl.BlockSpec((None, m, MIX_W), lambda bi, i: (bi, 0, 1)),
        ],
        out_specs=pl.BlockSpec((None, CA_TQ, MIX_W), lambda bi, i: (bi, i, 0)),
        out_shape=jax.ShapeDtypeStruct((b, s, MIX_W), BF16),
        compiler_params=pltpu.CompilerParams(
            dimension_semantics=("parallel", "parallel"), vmem_limit_bytes=VMEM_LIMIT),
        name="cross_attention",
    )(proj3, mkv3, mkv3)


MERGE_TM = 512
ROUTE_W = 128


def _top2_of_4(r0, r1, r2, r3):
    hi1, lo1 = jnp.maximum(r0, r1), jnp.minimum(r0, r1)
    hi2, lo2 = jnp.maximum(r2, r3), jnp.minimum(r2, r3)
    return jnp.maximum(hi1, hi2), jnp.maximum(jnp.minimum(hi1, hi2), jnp.maximum(lo1, lo2))


def _argmax_first(vals):
    best_v, best_i = vals[0], jnp.zeros_like(vals[0])
    for i in range(1, len(vals)):
        upd = vals[i] > best_v
        best_i = jnp.where(upd, float(i), best_i)
        best_v = jnp.where(upd, vals[i], best_v)
    return best_i


def _pick(idx, vals):
    out = vals[0]
    for i in range(1, len(vals)):
        out = jnp.where(idx == float(i), vals[i], out)
    return out


def _route(logits_t, bias_col):
    scores = jax.nn.sigmoid(logits_t)
    sel = scores + bias_col
    sel_rows = [sel[e:e + 1, :] for e in range(N_EXPERTS)]
    score_rows = [scores[e:e + 1, :] for e in range(N_EXPERTS)]
    grp = []
    for g in range(N_GROUPS):
        m1, m2 = _top2_of_4(*sel_rows[4 * g:4 * g + 4])
        grp.append(m1 + m2)
    best = _argmax_first(grp)
    cand = [_pick(best, [sel_rows[4 * g + i] for g in range(N_GROUPS)])
            for i in range(EXPERTS_PER_GROUP)]
    cand_score = [_pick(best, [score_rows[4 * g + i] for g in range(N_GROUPS)])
                  for i in range(EXPERTS_PER_GROUP)]
    i1 = _argmax_first(cand)
    i2 = _argmax_first([jnp.where(i1 == float(i), NEG_INF, cand[i])
                        for i in range(EXPERTS_PER_GROUP)])
    w1, w2 = _pick(i1, cand_score), _pick(i2, cand_score)
    tot = w1 + w2
    return w1 / tot, w2 / tot, best * 4.0 + i1, best * 4.0 + i2


def _merge_body(ya_ref, yb_ref, yc_ref, ga_ref, gb_ref, gc_ref, x_ref, wb_ref, wo_ref,
                gffn_ref, wr_hi_ref, wr_lo_ref, rbias_ref, tri_ref,
                x1_ref, h2_ref, route_ref, route_rows_ref, counts_ref):
    halves = [slice(0, MERGE_TM // 2), slice(MERGE_TM // 2, MERGE_TM)]
    mixers = ((ya_ref, ga_ref), (yb_ref, gb_ref), (yc_ref, gc_ref))
    branch = [[_dot(y_ref[rows, :], wb_ref[i]) for i, (y_ref, _) in enumerate(mixers)]
              for rows in halves]
    merged = [sum(jax.nn.sigmoid(gate_ref[rows, :].astype(F32)) * branch[hf][i]
                  for i, (_, gate_ref) in enumerate(mixers))
              for hf, rows in enumerate(halves)]
    x1 = [x_ref[rows, :] + _dot(merged[hf].astype(BF16), wo_ref[...])
          for hf, rows in enumerate(halves)]
    logits = []
    for hf, rows in enumerate(halves):
        x1_ref[rows, :] = x1[hf]
        h2 = _rms(x1[hf], gffn_ref[...])
        h2_ref[rows, :] = h2.astype(BF16)
        h_hi, h_lo = _split_bf16(h2)
        logits.append(_dot(h_hi, wr_hi_ref[...])
                      + (_dot(h_hi, wr_lo_ref[...]) + _dot(h_lo, wr_hi_ref[...])))
    logits_t = jnp.concatenate(logits, axis=0).T[0:N_EXPERTS, :]
    w1, w2, e1, e2 = _route(logits_t, rbias_ref[...])
    expert = lax.broadcasted_iota(jnp.int32, (N_EXPERTS, MERGE_TM), 0).astype(F32)
    pick1 = jnp.where(expert == e1, 1.0, 0.0)
    pick2 = jnp.where(expert == e2, 1.0, 0.0)
    picked = pick1 + pick2
    before = _dot(picked.astype(BF16), tri_ref[...])
    r1 = jnp.sum(pick1 * before, axis=0, keepdims=True)
    r2 = jnp.sum(pick2 * before, axis=0, keepdims=True)
    counts_ref[...] = jnp.broadcast_to(jnp.sum(picked, axis=1, keepdims=True),
                                       counts_ref.shape)
    fields = (w1, w2, e1, e2, r1, r2)
    row = lax.broadcasted_iota(jnp.int32, (ROUTE_W, MERGE_TM), 0)
    packed = jnp.zeros((ROUTE_W, MERGE_TM), F32)
    for i, f in enumerate(fields):
        packed = jnp.where(row == i, f, packed)
    route_ref[...] = packed.T
    route_rows_ref[...] = packed[0:8, :]


def _merge(ya, yb, yc, proj, x2d, w_branch, w_out, g_ffn, wr_hi, wr_lo, rbias):
    n = x2d.shape[0]
    tm = MERGE_TM

    def rows(width, cblk=0):
        return pl.BlockSpec((tm, width), lambda i: (i, cblk))

    def whole(shape):
        return pl.BlockSpec(shape, lambda i: (0,) * len(shape), pipeline_mode=pl.Buffered(1))

    return pl.pallas_call(
        _merge_body,
        grid=(n // tm,),
        in_specs=[rows(MIX_W), rows(MIX_W), rows(MIX_W),
                  rows(D_MODEL, COL_GATES // D_MODEL), rows(D_MODEL, COL_GATES // D_MODEL + 1),
                  rows(D_MODEL, COL_GATES // D_MODEL + 2),
                  rows(D_MODEL),
                  whole(w_branch.shape), whole(w_out.shape), whole((1, D_MODEL)),
                  whole(wr_hi.shape), whole(wr_lo.shape), whole((N_EXPERTS, 1)),
                  whole((tm, tm))],
        out_specs=[rows(D_MODEL), rows(D_MODEL), rows(ROUTE_W),
                   pl.BlockSpec((None, 8, tm), lambda i: (i, 0, 0)),
                   pl.BlockSpec((None, N_EXPERTS, ROUTE_W), lambda i: (i, 0, 0))],
        out_shape=[jax.ShapeDtypeStruct((n, D_MODEL), F32),
                   jax.ShapeDtypeStruct((n, D_MODEL), BF16),
                   jax.ShapeDtypeStruct((n, ROUTE_W), F32),
                   jax.ShapeDtypeStruct((n // tm, 8, tm), F32),
                   jax.ShapeDtypeStruct((n // tm, N_EXPERTS, ROUTE_W), F32)],
        compiler_params=pltpu.CompilerParams(
            dimension_semantics=("parallel",), vmem_limit_bytes=VMEM_LIMIT),
        name="merge_route",
    )(ya, yb, yc, proj, proj, proj, x2d, w_branch, w_out, g_ffn.reshape(1, D_MODEL), wr_hi, wr_lo,
      rbias.reshape(N_EXPERTS, 1),
      jnp.asarray(np.triu(np.ones((tm, tm), np.float32), 1), BF16))


PIECE = 16
BLOCK_ROWS = -(-(2 * MERGE_TM + N_EXPERTS * (PIECE - 1)) // 256) * 256
BLOCK_PIECES = BLOCK_ROWS // PIECE
FFN_TM = 512
XS_W = D_MODEL + 128
TAIL_PIECES = FFN_TM // PIECE


def _max_ffn_tiles(n_blocks):
    rows = n_blocks * (2 * MERGE_TM + N_EXPERTS * (PIECE - 1)) + N_EXPERTS * (FFN_TM - PIECE)
    return -(-rows // FFN_TM)


def _dispatch_plan(counts_out):
    i32 = jnp.int32
    counts = counts_out[:, :, 0].astype(i32)
    n_blocks = counts.shape[0]
    padded = (counts + PIECE - 1) // PIECE * PIECE
    loc = jnp.cumsum(padded, axis=1) - padded
    tot = jnp.sum(padded, axis=0)
    region = (tot + FFN_TM - 1) // FFN_TM * FFN_TM
    off = jnp.cumsum(region) - region
    seg = off[None, :] + jnp.cumsum(padded, axis=0) - padded
    experts = jnp.arange(N_EXPERTS, dtype=i32)

    def pick(table, e_idx):
        return jnp.sum(jnp.where(e_idx[..., None] == experts, table, 0), axis=-1, dtype=i32)

    k_row = jnp.arange(BLOCK_PIECES, dtype=i32) * PIECE
    e_of = jnp.sum(k_row[None, :, None] >= (loc + padded)[:, None, :], axis=2, dtype=i32)
    dest = pick((seg - loc)[:, None, :], jnp.minimum(e_of, N_EXPERTS - 1)) + k_row[None, :]
    n_tail = (region - tot) // PIECE
    tail_end = jnp.cumsum(n_tail)
    j = jnp.arange(N_EXPERTS * TAIL_PIECES, dtype=i32)
    e_tail = jnp.minimum(jnp.sum(j[:, None] >= tail_end[None, :], axis=1, dtype=i32),
                         N_EXPERTS - 1)
    zero_dest = pick((off + tot - (tail_end - n_tail) * PIECE)[None, :], e_tail) + j * PIECE
    tile_row = jnp.arange(_max_ffn_tiles(n_blocks), dtype=i32) * FFN_TM
    tile_expert = jnp.sum(tile_row[:, None] >= (off + region)[None, :], axis=1, dtype=i32)
    return dict(
        loc=loc.reshape(-1), dest=dest.reshape(-1),
        n_pieces=(jnp.sum(padded, axis=1) // PIECE).astype(i32),
        zero_dest=zero_dest, n_zero=tail_end[-1:].astype(i32),
        tile_expert=jnp.minimum(tile_expert, N_EXPERTS - 1),
        n_tiles=(jnp.sum(region) // FFN_TM).astype(i32).reshape(1))


def _block_row(expert, rank, loc_ref, block):
    start = jnp.zeros_like(rank)
    for e in range(N_EXPERTS):
        start = jnp.where(expert == float(e), loc_ref[block * N_EXPERTS + e].astype(F32), start)
    return start + rank


def _row_tags(cols):
    lanes = []
    for w_col in (cols[:, 0:1], cols[:, 1:2]):
        hi = w_col.astype(BF16).astype(F32)
        mid = (w_col - hi).astype(BF16).astype(F32)
        lanes += [hi, mid, w_col - hi - mid]
    lanes += [cols[:, 2:3], cols[:, 3:4]]
    lane = lax.broadcasted_iota(jnp.int32, (cols.shape[0], 128), 1)
    tags = jnp.zeros((cols.shape[0], 128), F32)
    for i, v in enumerate(lanes):
        tags = jnp.where(lane == i, v, tags)
    return tags.astype(BF16)


def _for_each(n, fn, group=1):
    def trip(i, c):
        for u in range(group):
            fn(i * group + u)
        return c

    lax.fori_loop(0, n // group, trip, 0)
    if group > 1:
        lax.fori_loop(n // group * group, n, lambda k, c: (fn(k), c)[1], 0)


def _dispatch_body(loc_ref, dest_ref, npieces_ref, zdest_ref, nzero_ref, ntiles_ref,
                   rows_ref, cols_ref, h_ref, xs_hbm, buf_ref, zero_ref, sem, zero_sem):
    b = pl.program_id(0)
    slot = b % 2

    def piece_copy(k, block, s):
        src = buf_ref.at[s, pl.ds(pl.multiple_of(k * PIECE, PIECE), PIECE), :]
        dst = xs_hbm.at[pl.ds(pl.multiple_of(dest_ref[block * BLOCK_PIECES + k], PIECE), PIECE), :]
        return pltpu.make_async_copy(src, dst, sem.at[s])

    def zero_copy(j):
        dst = xs_hbm.at[pl.ds(pl.multiple_of(zdest_ref[j], PIECE), PIECE), :]
        return pltpu.make_async_copy(zero_ref.at[pl.ds(0, PIECE), :], dst, zero_sem)

    def zero_tile_copy(t):
        dst = xs_hbm.at[pl.ds(pl.multiple_of(t * FFN_TM, FFN_TM), FFN_TM), :]
        return pltpu.make_async_copy(zero_ref, dst, zero_sem)

    idle_tiles = xs_hbm.shape[0] // FFN_TM - ntiles_ref[0]

    @pl.when(b == 0)
    def _start_zero_fill():
        zero_ref[...] = jnp.zeros_like(zero_ref)
        _for_each(nzero_ref[0], lambda j: zero_copy(j).start())
        _for_each(idle_tiles, lambda t: zero_tile_copy(ntiles_ref[0] + t).start())

    idx1 = _block_row(rows_ref[2:3, :], rows_ref[4:5, :], loc_ref, b)
    idx2 = _block_row(rows_ref[3:4, :], rows_ref[5:6, :], loc_ref, b)
    row = lax.broadcasted_iota(jnp.int32, (BLOCK_ROWS, MERGE_TM), 0).astype(F32)
    permute = jnp.where((row == idx1) | (row == idx2), 1.0, 0.0).astype(BF16)
    buf_ref[slot, :, 0:D_MODEL] = _dot(permute, h_ref[...]).astype(BF16)
    buf_ref[slot, :, D_MODEL:XS_W] = _dot(permute, _row_tags(cols_ref[...])).astype(BF16)
    _for_each(npieces_ref[b], lambda k: piece_copy(k, b, slot).start(), group=4)

    def wait_block(block, s):
        pltpu.make_async_copy(buf_ref.at[s, pl.ds(0, 2 * MERGE_TM), :],
                              xs_hbm.at[pl.ds(0, 2 * MERGE_TM), :], sem.at[s]).wait()
        _for_each(npieces_ref[block] - 2 * MERGE_TM // PIECE,
                  lambda k: piece_copy(k, block, s).wait())

    @pl.when(b > 0)
    def _wait_previous_block():
        wait_block(b - 1, 1 - slot)

    @pl.when(b == pl.num_programs(0) - 1)
    def _wait_last_block():
        wait_block(b, slot)

    @pl.when(b == 0)
    def _wait_zero_fill():
        _for_each(nzero_ref[0], lambda j: zero_copy(j).wait())
        _for_each(idle_tiles, lambda t: zero_tile_copy(ntiles_ref[0] + t).wait())


def _dispatch(plan, route_rows, route, h2):
    n_blocks = route_rows.shape[0]
    rows_max = _max_ffn_tiles(n_blocks) * FFN_TM
    return pl.pallas_call(
        _dispatch_body,
        grid_spec=pltpu.PrefetchScalarGridSpec(
            num_scalar_prefetch=6,
            grid=(n_blocks,),
            in_specs=[pl.BlockSpec((None, 8, MERGE_TM), lambda i, *_: (i, 0, 0)),
                      pl.BlockSpec((MERGE_TM, ROUTE_W), lambda i, *_: (i, 0)),
                      pl.BlockSpec((MERGE_TM, D_MODEL), lambda i, *_: (i, 0))],
            out_specs=pl.BlockSpec(memory_space=pl.ANY),
            scratch_shapes=[pltpu.VMEM((2, BLOCK_ROWS, XS_W), BF16),
                            pltpu.VMEM((FFN_TM, XS_W), BF16),
                            pltpu.SemaphoreType.DMA((2,)),
                            pltpu.SemaphoreType.DMA(())]),
        out_shape=jax.ShapeDtypeStruct((rows_max, XS_W), BF16),
        compiler_params=pltpu.CompilerParams(
            dimension_semantics=("arbitrary",), vmem_limit_bytes=VMEM_LIMIT),
        name="moe_dispatch",
    )(plan["loc"], plan["dest"], plan["n_pieces"], plan["zero_dest"], plan["n_zero"],
      plan["n_tiles"], route_rows, route, h2)


def _ffn_body(tile_expert_ref, ntiles_ref, xs_ref, wg_ref, wu_ref, wd_ref, ys_ref,
              wg_bf_ref, wu_bf_ref, wd_bf_ref):
    i = pl.program_id(0)
    active = i < ntiles_ref[0]
    expert = tile_expert_ref[i]

    @pl.when(jnp.logical_not(active))
    def _idle_tile():
        ys_ref[...] = jnp.zeros_like(ys_ref)

    @pl.when(active & ((i == 0) | (expert != tile_expert_ref[jnp.maximum(i - 1, 0)])))
    def _cast_expert_weights():
        wg_bf_ref[...] = wg_ref[...].astype(BF16)
        wu_bf_ref[...] = wu_ref[...].astype(BF16)
        wd_bf_ref[...] = wd_ref[...].astype(BF16)

    @pl.when(active)
    def _run_tile():
        halves = [slice(0, FFN_TM // 2), slice(FFN_TM // 2, FFN_TM)]
        gate_up = [(_dot(xs_ref[rows, 0:D_MODEL], wg_bf_ref[...]),
                    _dot(xs_ref[rows, 0:D_MODEL], wu_bf_ref[...])) for rows in halves]
        for rows, (g, u) in zip(halves, gate_up):
            tags = xs_ref[rows, D_MODEL:XS_W].astype(F32)
            first_pick = tags[:, 6:7] == expert.astype(F32)
            weight = jnp.where(first_pick, tags[:, 0:1] + tags[:, 1:2] + tags[:, 2:3],
                               tags[:, 3:4] + tags[:, 4:5] + tags[:, 5:6])
            a = jax.nn.silu(g) * u * weight
            ys_ref[rows, :] = _dot(a.astype(BF16), wd_bf_ref[...]).astype(BF16)


def _expert_ffn(plan, xs, wg, wu, wd, layer):
    n_tiles_max = xs.shape[0] // FFN_TM

    def tile(i, tile_expert, n_tiles):
        return jnp.minimum(i, n_tiles[0] - 1)

    def expert(shape):
        return pl.BlockSpec((None, None) + shape,
                            lambda i, te, nt: (layer, te[tile(i, te, nt)], 0, 0))

    return pl.pallas_call(
        _ffn_body,
        grid_spec=pltpu.PrefetchScalarGridSpec(
            num_scalar_prefetch=2,
            grid=(n_tiles_max,),
            in_specs=[pl.BlockSpec((FFN_TM, XS_W), lambda i, te, nt: (tile(i, te, nt), 0)),
                      expert((D_MODEL, D_EXPERT)), expert((D_MODEL, D_EXPERT)),
                      expert((D_EXPERT, D_MODEL))],
            out_specs=pl.BlockSpec((FFN_TM, D_MODEL), lambda i, te, nt: (i, 0)),
            scratch_shapes=[pltpu.VMEM((D_MODEL, D_EXPERT), BF16),
                            pltpu.VMEM((D_MODEL, D_EXPERT), BF16),
                            pltpu.VMEM((D_EXPERT, D_MODEL), BF16)]),
        out_shape=jax.ShapeDtypeStruct((xs.shape[0], D_MODEL), BF16),
        compiler_params=pltpu.CompilerParams(
            dimension_semantics=("arbitrary",), vmem_limit_bytes=VMEM_LIMIT),
        name="moe_expert_ffn",
    )(plan["tile_expert"], plan["n_tiles"], xs, wg, wu, wd)


def _combine_body(loc_ref, dest_ref, npieces_ref, cols_ref, x1_ref, gfin_ref, ys_hbm, o_ref,
                  buf_ref, sem, *, final_norm):
    b = pl.program_id(0)
    slot = b % 2

    def piece_copy(k, block, s):
        src = ys_hbm.at[pl.ds(pl.multiple_of(dest_ref[block * BLOCK_PIECES + k], PIECE), PIECE), :]
        dst = buf_ref.at[s, pl.ds(pl.multiple_of(k * PIECE, PIECE), PIECE), :]
        return pltpu.make_async_copy(src, dst, sem.at[s])

    @pl.when(b == 0)
    def _first_block():
        buf_ref[...] = jnp.zeros_like(buf_ref)
        _for_each(npieces_ref[0], lambda k: piece_copy(k, 0, 0).start(), group=4)

    @pl.when(b + 1 < pl.num_programs(0))
    def _prefetch_next_block():
        _for_each(npieces_ref[b + 1], lambda k: piece_copy(k, b + 1, 1 - slot).start(),
                  group=4)

    pltpu.make_async_copy(ys_hbm.at[pl.ds(0, 2 * MERGE_TM), :],
                          buf_ref.at[slot, pl.ds(0, 2 * MERGE_TM), :], sem.at[slot]).wait()
    _for_each(npieces_ref[b] - 2 * MERGE_TM // PIECE, lambda k: piece_copy(k, b, slot).wait())

    idx1 = _block_row(cols_ref[:, 2:3], cols_ref[:, 4:5], loc_ref, b)
    idx2 = _block_row(cols_ref[:, 3:4], cols_ref[:, 5:6], loc_ref, b)
    col = lax.broadcasted_iota(jnp.int32, (MERGE_TM, BLOCK_ROWS), 1).astype(F32)
    unpermute = jnp.where((col == idx1) | (col == idx2), 1.0, 0.0).astype(BF16)
    out = x1_ref[...] + _dot(unpermute, buf_ref[slot])
    o_ref[...] = _rms(out, gfin_ref[...]) if final_norm else out


def _combine(plan, route, x1, ys, g_final, final_norm):
    n = x1.shape[0]
    return pl.pallas_call(
        functools.partial(_combine_body, final_norm=final_norm),
        grid_spec=pltpu.PrefetchScalarGridSpec(
            num_scalar_prefetch=3,
            grid=(n // MERGE_TM,),
            in_specs=[pl.BlockSpec((MERGE_TM, ROUTE_W), lambda i, *_: (i, 0)),
                      pl.BlockSpec((MERGE_TM, D_MODEL), lambda i, *_: (i, 0)),
                      pl.BlockSpec((1, D_MODEL), lambda i, *_: (0, 0)),
                      pl.BlockSpec(memory_space=pl.ANY)],
            out_specs=pl.BlockSpec((MERGE_TM, D_MODEL), lambda i, *_: (i, 0)),
            scratch_shapes=[pltpu.VMEM((2, BLOCK_ROWS, D_MODEL), BF16),
                            pltpu.SemaphoreType.DMA((2,))]),
        out_shape=jax.ShapeDtypeStruct((n, D_MODEL), F32),
        compiler_params=pltpu.CompilerParams(
            dimension_semantics=("arbitrary",), vmem_limit_bytes=VMEM_LIMIT),
        name="moe_combine",
    )(plan["loc"], plan["dest"], plan["n_pieces"], route, x1, g_final.reshape(1, D_MODEL), ys)


def kernel(x, mem, g_mix, w_in, da_lambda, da_sub_g, hg_lower_bounds, hg_out_g, g_mem, w_mem_kv, w_branch, w_out, g_ffn, w_router, router_bias, w_exp_gate, w_exp_up, w_exp_down, g_final):
    b, s, d = x.shape
    m = mem.shape[1]
    n = b * s
    wr_pad = jnp.pad(w_router.astype(F32), ((0, 0), (0, ROUTE_W - N_EXPERTS)))
    wr_hi, wr_lo = _split_bf16(wr_pad)
    x2d = x.reshape(n, d)
    mem2d = mem.reshape(b * m, d)
    for l in range(DEPTH):
        lam_init = 0.8 - 0.6 * math.exp(-0.3 * l)
        proj = _norm_proj(x2d, g_mix[l], w_in[l].astype(BF16), row_tile=512, name="in_proj")
        proj3 = proj.reshape(b, s, IN_TOTAL)
        mkv = _norm_proj(mem2d, g_mem[l], w_mem_kv[l].astype(BF16), row_tile=512,
                         name="mem_kv_proj")
        y_a = _diff_attention(proj3, da_lambda[l].astype(F32), da_sub_g[l], lam_init)
        y_b = _hgrn2(proj3, hg_lower_bounds.astype(F32), hg_out_g[l], l)
        y_c = _cross_attention(proj3, mkv.reshape(b, m, 2 * MIX_W))
        x1, h2, route, route_rows, counts = _merge(
            y_a.reshape(n, MIX_W), y_b.reshape(n, MIX_W), y_c.reshape(n, MIX_W), proj, x2d,
            w_branch[l].astype(BF16), w_out[l].astype(BF16), g_ffn[l],
            wr_hi, wr_lo, router_bias.astype(F32))
        plan = _dispatch_plan(counts)
        xs = _dispatch(plan, route_rows, route, h2)
        ys = _expert_ffn(plan, xs, w_exp_gate, w_exp_up, w_exp_down, l)
        x2d = _combine(plan, route, x1, ys, g_final, final_norm=(l == DEPTH - 1))
    return x2d.reshape(b, s, d)
```
